```python
import jax
import jax.numpy as jnp
from jax import lax
import numpy as np

D_MODEL = 1024
BATCH = 2
SEQ = 8192
DEPTH = 2

CTX_LEN = 256
GRID_W = 64
CHUNK = 128
BRANCH_W = 512
N_BRANCH = 3
N_HEADS_RWKV = 8
HEAD_DIM_RWKV = BRANCH_W // N_HEADS_RWKV
DECAY_LORA = 64
AAA_LORA = 64
MV_LORA = 32
GATE_LORA = 128
N_HEADS_RET = 4
HEAD_DIM_RET = BRANCH_W // N_HEADS_RET
N_HEADS_HGRN = 4
HEAD_DIM_HGRN = BRANCH_W // N_HEADS_HGRN
D_FF = 4 * D_MODEL
ROPE_BASE = 10000.0
NORM_EPS = 1e-6
RWKV_LN_EPS = 64e-5
L2_EPS = 1e-12

RWKV_SIZES = (BRANCH_W, BRANCH_W, BRANCH_W, DECAY_LORA, DECAY_LORA, AAA_LORA, AAA_LORA, GATE_LORA)
N_RWKV_IN = 3 * BRANCH_W + 2 * DECAY_LORA + 2 * AAA_LORA + GATE_LORA
N_RET_IN = 4 * BRANCH_W
N_HGRN_IN = 5 * BRANCH_W
N_GATE_IN = N_BRANCH * D_MODEL
N_IN = N_RWKV_IN + N_RET_IN + N_HGRN_IN + N_GATE_IN

kernel_name = 'hybrid_rwkv7_retnet_hgrn2_prefix_dit'


def split_sizes(a, sizes):
    idx, acc = [], 0
    for s in sizes[:-1]:
        acc += s
        idx.append(acc)
    return jnp.split(a, idx, axis=-1)


def rms_norm(x, w=None):
    xf = x.astype(jnp.float32)
    y = xf * lax.rsqrt(jnp.mean(xf * xf, axis=-1, keepdims=True) + NORM_EPS)
    if w is not None:
        y = y * w.astype(jnp.float32)
    return y.astype(x.dtype)


def heads(a, n_heads):
    B, N, C = a.shape
    return a.reshape(B, N, n_heads, C // n_heads).transpose(0, 2, 1, 3)


def merge_heads(a):
    B, H, N, Dh = a.shape
    return a.transpose(0, 2, 1, 3).reshape(B, N, H * Dh)


def nbr_mean_seq(p):
    pp = jnp.pad(p, ((0, 0), (1, 1), (0, 0)))
    return 0.5 * (pp[:, :-2] + pp[:, 2:])


def nbr_mean_grid(p):
    B, T, C = p.shape
    rows = T // GRID_W
    g = jnp.pad(p.reshape(B, rows, GRID_W, C), ((0, 0), (1, 1), (1, 1), (0, 0)))
    m = 0.25 * (g[:, :-2, 1:-1] + g[:, 2:, 1:-1] + g[:, 1:-1, :-2] + g[:, 1:-1, 2:])
    return m.reshape(B, T, C)


def rope(x, pos):
    half = x.shape[-1] // 2
    inv_freq = ROPE_BASE ** (-jnp.arange(half, dtype=jnp.float32) / half)
    ang = pos.astype(jnp.float32)[:, None] * inv_freq[None, :]
    cos, sin = jnp.cos(ang).astype(x.dtype), jnp.sin(ang).astype(x.dtype)
    x1, x2 = x[..., :half], x[..., half:]
    return jnp.concatenate([x1 * cos - x2 * sin, x1 * sin + x2 * cos], axis=-1)


def chunk_linear_scan(q, k, v, log_f, s0):
    dt = q.dtype
    B, H, N, _ = q.shape
    n_blocks = N // CHUNK
    f32 = jnp.float32

    def blocks(a):
        return jnp.moveaxis(a.astype(f32).reshape(B, H, n_blocks, CHUNK, a.shape[-1]), 2, 0)

    lower = jnp.tril(jnp.ones((CHUNK, CHUNK), dtype=bool))[:, :, None]

    def step(S, blk):
        qc, kc, vc, gc = blk
        G = jnp.cumsum(gc, axis=2)
        rel = jnp.where(lower, G[:, :, :, None, :] - G[:, :, None, :, :], -jnp.inf)
        if gc.shape[-1] == 1:
            A = jnp.einsum('bhtd,bhsd->bhts', qc, kc) * jnp.exp(rel[..., 0])
        else:
            A = jnp.einsum('bhtd,bhsd,bhtsd->bhts', qc, kc, jnp.exp(rel))
        o = A @ vc + (qc * jnp.exp(G)) @ S
        G_end = G[:, :, -1:, :]
        S = jnp.exp(G_end).swapaxes(-1, -2) * S + (kc * jnp.exp(G_end - G)).swapaxes(-1, -2) @ vc
        return S, o

    S, o = lax.scan(step, s0.astype(f32), (blocks(q), blocks(k), blocks(v), blocks(log_f)))
    return jnp.moveaxis(o, 0, 2).reshape(B, H, N, -1).astype(dt), S


def rwkv7_scan(r, log_w, k, v, kk, a, s0):
    dt = r.dtype
    f32 = jnp.float32

    def step(S, inp):
        r_t, lw_t, k_t, v_t, kk_t, a_t = inp
        s_kk = jnp.einsum('bhvk,bhk->bhv', S, kk_t)
        S = (S * jnp.exp(lw_t)[:, :, None, :]
             - s_kk[..., None] * (a_t * kk_t)[:, :, None, :]
             + v_t[..., None] * k_t[:, :, None, :])
        return S, jnp.einsum('bhvk,bhk->bhv', S, r_t)

    xs = tuple(jnp.moveaxis(t.astype(f32), 2, 0) for t in (r, log_w, k, v, kk, a))
    S, o = lax.scan(step, s0.astype(f32), xs)
    return jnp.moveaxis(o, 0, 2).astype(dt), S


def bidir_prefix(scan_fn, fwd, bwd, s0, n_ctx):
    cf = tuple(a[:, :, :n_ctx] for a in fwd)
    lf = tuple(a[:, :, n_ctx:] for a in fwd)
    cb = tuple(jnp.flip(a[:, :, :n_ctx], axis=2) for a in bwd)
    lb = tuple(jnp.flip(a[:, :, n_ctx:], axis=2) for a in bwd)
    o_cf, s_cf = scan_fn(*cf, s0)
    o_lf, _ = scan_fn(*lf, s_cf)
    o_cb, s_cb = scan_fn(*cb, s0)
    o_lb, _ = scan_fn(*lb, s_cb)
    return jnp.concatenate([o_cf + jnp.flip(o_cb, axis=2), o_lf + jnp.flip(o_lb, axis=2)], axis=2)


def rwkv7_mixer(p, h, v_first, vres, n_ctx, mu, w0, w_up, a0, a_up, g_up, k_k, k_a, r_k, lnx_w, lnx_b):
    B = p.shape[0]
    H = N_HEADS_RWKV
    f32 = jnp.float32
    shifted = jnp.concatenate([nbr_mean_seq(p[:, :n_ctx]), nbr_mean_grid(p[:, n_ctx:])], axis=1)
    p = p + mu * (shifted - p)
    r, k, v, wd_f, wd_b, ad_f, ad_b, gd = split_sizes(p, RWKV_SIZES)
    if vres is None:
        v_first = v
    else:
        v0, v_down, v_up = vres
        v = v + (v_first - v) * jax.nn.sigmoid(v0 + (h @ v_down) @ v_up)
    g = jax.nn.sigmoid(gd) @ g_up
    r_h, v_h = heads(r, H), heads(v, H)
    kk = heads(k * k_k, H).astype(f32)
    kk = kk * lax.rsqrt(jnp.sum(kk * kk, axis=-1, keepdims=True) + L2_EPS)
    r_k_h = r_k.reshape(H, 1, HEAD_DIM_RWKV)
    dirs = []
    for d, (wd, ad) in enumerate(((wd_f, ad_f), (wd_b, ad_b))):
        w_raw = -jax.nn.softplus(-(w0[d] + jnp.tanh(wd) @ w_up[d])) - 0.5
        a = jax.nn.sigmoid(a0[d] + ad @ a_up[d])
        k_d = heads(k * (1.0 + (a - 1.0) * k_a), H)
        dirs.append((r_h, heads(-jnp.exp(w_raw.astype(f32)), H), k_d, v_h, kk, heads(a, H)))
    s0 = jnp.zeros((B, H, HEAD_DIM_RWKV, HEAD_DIM_RWKV), f32)
    o = bidir_prefix(rwkv7_scan, dirs[0], dirs[1], s0, n_ctx).astype(f32)
    mean = jnp.mean(o, axis=-1, keepdims=True)
    var = jnp.mean(jnp.square(o - mean), axis=-1, keepdims=True)
    o = merge_heads((o - mean) * lax.rsqrt(var + RWKV_LN_EPS)) * lnx_w + lnx_b
    bonus = (jnp.sum(r_h * dirs[0][2] * r_k_h, axis=-1, keepdims=True)
             + jnp.sum(r_h * dirs[1][2] * r_k_h, axis=-1, keepdims=True)) * v_h
    y = (o + merge_heads(bonus)).astype(p.dtype) * g
    return y, v_first


def retention_mixer(p, pos, decay_param, n_ctx):
    H = N_HEADS_RET
    q, k, v, g = split_sizes(p, (BRANCH_W,) * 4)
    q_h = rope(heads(q, H), pos)
    k_h = rope(heads(k, H), pos) * (HEAD_DIM_RET ** -0.5)
    v_h = heads(v, H)
    B, _, N, _ = q_h.shape
    log_gamma = -jnp.exp(decay_param.astype(jnp.float32))
    lf = [jnp.broadcast_to(log_gamma[d][None, :, None, None], (B, H, N, 1)) for d in range(2)]
    s0 = jnp.zeros((B, H, HEAD_DIM_RET, HEAD_DIM_RET), jnp.float32)
    o = bidir_prefix(chunk_linear_scan, (q_h, k_h, v_h, lf[0]), (q_h, k_h, v_h, lf[1]), s0, n_ctx)
    o = rms_norm(o)
    return merge_heads(o) * jax.nn.silu(g)


def hgrn2_mixer(p, lb, norm_w, n_ctx):
    H = N_HEADS_HGRN
    q, f_f, f_b, i, g = split_sizes(p, (BRANCH_W,) * 5)
    q_h, i_h = heads(jax.nn.silu(q), H), heads(i, H)
    B, _, N, _ = q_h.shape
    dirs = []
    for d, f in enumerate((f_f, f_b)):
        f32f = f.astype(jnp.float32)
        forget = lb[d] + (1.0 - lb[d]) * jax.nn.sigmoid(f32f)
        inp_gate = (1.0 - lb[d]) * jax.nn.sigmoid(-f32f)
        dirs.append((q_h, heads(inp_gate, H), i_h, heads(jnp.log(forget), H)))
    s0 = jnp.zeros((B, H, HEAD_DIM_HGRN, HEAD_DIM_HGRN), jnp.float32)
    o = bidir_prefix(chunk_linear_scan, dirs[0], dirs[1], s0, n_ctx)
    o = rms_norm(o, norm_w.reshape(H, 1, HEAD_DIM_HGRN))
    return merge_heads(o) * jax.nn.silu(g)


def sq_relu_mlp(h, w1, w2):
    return jnp.square(jax.nn.relu(h @ w1)) @ w2


def setup_inputs(seed: int = 0) -> dict:
    key = jax.random.key(seed)
    ks = jax.random.split(key, 32)
    f32 = jnp.float32
    L, D = DEPTH, D_MODEL

    def nrm(i, shape, s):
        return s * jax.random.normal(ks[i], shape, f32)

    ret_base = jnp.asarray(np.log(-np.log1p(-2.0 ** (-5.0 - np.arange(N_HEADS_RET)))), f32)
    return {
        'x': nrm(0, (BATCH, SEQ, D), 1.0),
        'c': nrm(1, (BATCH, D), 1.0),
        'ctx': nrm(2, (BATCH, CTX_LEN, D), 1.0),
        'c_ctx': nrm(3, (D,), 1.0),
        'ada_w': nrm(4, (L, D, 6 * D), 0.5 * D ** -0.5),
        'ada_b': nrm(5, (L, 6 * D), 0.01),
        'norm1_w': 1.0 + nrm(6, (L, D), 0.05),
        'norm2_w': 1.0 + nrm(7, (L, D), 0.05),
        'w_in': nrm(8, (L, D, N_IN), D ** -0.5),
        'rwkv_mu': jax.random.uniform(ks[9], (L, N_RWKV_IN), f32),
        'rwkv_w0': jax.random.uniform(ks[10], (L, 2, BRANCH_W), f32, minval=-6.0, maxval=-1.0),
        'rwkv_w_up': nrm(11, (L, 2, DECAY_LORA, BRANCH_W), 0.1 * DECAY_LORA ** -0.5),
        'rwkv_a0': nrm(12, (L, 2, BRANCH_W), 0.5),
        'rwkv_a_up': nrm(13, (L, 2, AAA_LORA, BRANCH_W), 0.1 * AAA_LORA ** -0.5),
        'rwkv_g_up': nrm(14, (L, GATE_LORA, BRANCH_W), GATE_LORA ** -0.5),
        'rwkv_k_k': 0.85 + nrm(15, (L, BRANCH_W), 0.05),
        'rwkv_k_a': 1.0 + nrm(16, (L, BRANCH_W), 0.05),
        'rwkv_r_k': nrm(17, (L, BRANCH_W), 0.1),
        'rwkv_lnx_w': 1.0 + nrm(18, (L, BRANCH_W), 0.05),
        'rwkv_lnx_b': nrm(19, (L, BRANCH_W), 0.01),
        'rwkv_v0': 1.0 + nrm(20, (L - 1, BRANCH_W), 0.1),
        'rwkv_v_down': nrm(21, (L - 1, D, MV_LORA), D ** -0.5),
        'rwkv_v_up': nrm(22, (L - 1, MV_LORA, BRANCH_W), 0.1 * MV_LORA ** -0.5),
        'ret_decay': ret_base + nrm(23, (L, 2, N_HEADS_RET), 0.05),
        'hgrn_lb': nrm(24, (L, 2, BRANCH_W), 0.5),
        'hgrn_norm_w': 1.0 + nrm(25, (L, BRANCH_W), 0.05),
        'w_branch': nrm(26, (L, N_BRANCH, BRANCH_W, D), BRANCH_W ** -0.5),
        'w_out': nrm(27, (L, D, D), D ** -0.5),
        'mlp_w1': nrm(28, (L, D, D_FF), D ** -0.5),
        'mlp_w2': nrm(29, (L, D_FF, D), D_FF ** -0.5),
        'final_norm_w': 1.0 + nrm(30, (D,), 0.05),
    }


def reference(x, c, ctx, c_ctx, ada_w, ada_b, norm1_w, norm2_w, w_in, rwkv_mu, rwkv_w0, rwkv_w_up,
              rwkv_a0, rwkv_a_up, rwkv_g_up, rwkv_k_k, rwkv_k_a, rwkv_r_k, rwkv_lnx_w, rwkv_lnx_b,
              rwkv_v0, rwkv_v_down, rwkv_v_up, ret_decay, hgrn_lb, hgrn_norm_w, w_branch, w_out,
              mlp_w1, mlp_w2, final_norm_w):
    n_ctx = ctx.shape[1]
    pos = jnp.arange(n_ctx + x.shape[1])
    sm = jax.nn.softmax(hgrn_lb.astype(jnp.float32), axis=0)
    hgrn_lower = jnp.cumsum(sm, axis=0) - sm[0:1]
    x_c, x_l = ctx, x
    v_first = None
    for l in range(DEPTH):
        last = l == DEPTH - 1
        mod_l = jax.nn.silu(c) @ ada_w[l] + ada_b[l]
        mod_c = jax.nn.silu(c_ctx) @ ada_w[l] + ada_b[l]
        sh1_l, sc1_l, g1_l, sh2_l, sc2_l, g2_l = (m[:, None, :] for m in jnp.split(mod_l, 6, axis=-1))
        sh1_c, sc1_c, g1_c, sh2_c, sc2_c, g2_c = jnp.split(mod_c, 6)
        h = jnp.concatenate([rms_norm(x_c, norm1_w[l]) * (1.0 + sc1_c) + sh1_c,
                             rms_norm(x_l, norm1_w[l]) * (1.0 + sc1_l) + sh1_l], axis=1)
        p_rw, p_ret, p_hg, p_gate = split_sizes(h @ w_in[l], (N_RWKV_IN, N_RET_IN, N_HGRN_IN, N_GATE_IN))
        vres = None if l == 0 else (rwkv_v0[l - 1], rwkv_v_down[l - 1], rwkv_v_up[l - 1])
        y_rw, v_first = rwkv7_mixer(p_rw, h, v_first, vres, n_ctx, rwkv_mu[l], rwkv_w0[l], rwkv_w_up[l],
                                    rwkv_a0[l], rwkv_a_up[l], rwkv_g_up[l], rwkv_k_k[l], rwkv_k_a[l],
                                    rwkv_r_k[l], rwkv_lnx_w[l], rwkv_lnx_b[l])
        y_ret = retention_mixer(p_ret, pos, ret_decay[l], n_ctx)
        y_hg = hgrn2_mixer(p_hg, hgrn_lower[l], hgrn_norm_w[l], n_ctx)
        if last:
            y_rw, y_ret, y_hg, p_gate = (t[:, n_ctx:] for t in (y_rw, y_ret, y_hg, p_gate))
        gate_rw, gate_ret, gate_hg = jnp.split(jax.nn.sigmoid(p_gate), N_BRANCH, axis=-1)
        merged = (gate_rw * (y_rw @ w_branch[l, 0]) + gate_ret * (y_ret @ w_branch[l, 1])
                  + gate_hg * (y_hg @ w_branch[l, 2]))
        out = merged @ w_out[l]
        if last:
            x_l = x_l + g1_l * out
        else:
            x_c = x_c + g1_c * out[:, :n_ctx]
            x_l = x_l + g1_l * out[:, n_ctx:]
        x_l = x_l + g2_l * sq_relu_mlp(rms_norm(x_l, norm2_w[l]) * (1.0 + sc2_l) + sh2_l, mlp_w1[l], mlp_w2[l])
        if not last:
            x_c = x_c + g2_c * sq_relu_mlp(rms_norm(x_c, norm2_w[l]) * (1.0 + sc2_c) + sh2_c, mlp_w1[l], mlp_w2[l])
    return rms_norm(x_l, final_norm_w)
```

```python
import functools

import jax
import jax.numpy as jnp
from jax import lax
from jax.experimental import pallas as pl
from jax.experimental.pallas import tpu as pltpu

F32 = jnp.float32
BF16 = jnp.bfloat16

NORM_EPS = 1e-6
RWKV_LN_EPS = 64e-5
L2_EPS = 1e-12
ROPE_BASE = 10000.0
GRID_W = 64

BRANCH_W = 512
RWKV_HEAD = 64
RET_HEAD = 128
HGRN_HEAD = 128
LORA_W = 384

RWKV_CHUNK = 64
HGRN_CHUNK = 64
HGRN_SUB = 16
RET_CHUNK = 128

VMEM_LIMIT = 56 * 1024 * 1024


def _sigmoid(x):
    return 1.0 / (1.0 + jnp.exp(-x))


def _dot(a, b):
    return jnp.dot(a.astype(BF16), b.astype(BF16), preferred_element_type=F32)


def _dot_nt(a, b):
    return lax.dot_general(a.astype(BF16), b.astype(BF16), (((1,), (1,)), ((), ())),
                           preferred_element_type=F32)


def _dot_tn(a, b):
    return lax.dot_general(a.astype(BF16), b.astype(BF16), (((0,), (0,)), ((), ())),
                           preferred_element_type=F32)


def _cumsum_rows(mask_f32, x):
    return jnp.dot(mask_f32, x, precision=lax.Precision.HIGHEST, preferred_element_type=F32)


def _ada_kernel(c_ref, w_ref, b_ref, o_ref):
    cv = c_ref[...]
    s = cv * _sigmoid(cv)
    o_ref[0] = _dot(s, w_ref[0]) + b_ref[0]


def _ada_call(cvec, ada_w, ada_b):
    L, D, D6 = ada_w.shape
    tn = D6 // 4
    return pl.pallas_call(
        _ada_kernel,
        grid=(L, D6 // tn),
        in_specs=[pl.BlockSpec((8, D), lambda l, j: (0, 0)),
                  pl.BlockSpec((1, D, tn), lambda l, j: (l, 0, j)),
                  pl.BlockSpec((1, 1, tn), lambda l, j: (l, 0, j))],
        out_specs=pl.BlockSpec((1, 8, tn), lambda l, j: (l, 0, j)),
        out_shape=jax.ShapeDtypeStruct((L, 8, D6), F32),
        compiler_params=pltpu.CompilerParams(vmem_limit_bytes=VMEM_LIMIT),
        name="ada_mod",
    )(cvec, ada_w, ada_b.reshape(L, 1, D6))


def _modulated_norm(xf, nw, mod_ref, is_ctx, which):
    y = xf * lax.rsqrt(jnp.mean(xf * xf, axis=-1, keepdims=True) + NORM_EPS) * nw
    sh = jnp.where(is_ctx, mod_ref[0, which:which + 1, :], mod_ref[0, 6 + which:7 + which, :])
    sc = jnp.where(is_ctx, mod_ref[0, which + 1:which + 2, :], mod_ref[0, 7 + which:8 + which, :])
    return y * (1.0 + sc) + sh


def _is_ctx_rows(tile_idx, tm, n_ctx):
    row = tile_idx * tm + lax.broadcasted_iota(jnp.int32, (tm, 1), 0)
    return row < n_ctx


def _proj_kernel(x_ref, mod_ref, nw_ref, w_ref, o_ref, h_ref, *, tm, n_ctx):
    i = pl.program_id(1)

    @pl.when(pl.program_id(2) == 0)
    def _():
        h = _modulated_norm(x_ref[0], nw_ref[...], mod_ref, _is_ctx_rows(i, tm, n_ctx), 0)
        h_ref[...] = h.astype(BF16)

    o_ref[0] = jnp.dot(h_ref[...], w_ref[...], preferred_element_type=F32)


def _proj_call(xa, modsel, nw, w, n_ctx, tm=768, n_tiles=4):
    B, N, D = xa.shape
    NP = w.shape[1]
    tn = NP // n_tiles
    return pl.pallas_call(
        functools.partial(_proj_kernel, tm=tm, n_ctx=n_ctx),
        grid=(B, N // tm, n_tiles),
        in_specs=[pl.BlockSpec((1, tm, D), lambda b, i, j: (b, i, 0)),
                  pl.BlockSpec((1, 12, D), lambda b, i, j: (b, 0, 0)),
                  pl.BlockSpec((1, D), lambda b, i, j: (0, 0)),
                  pl.BlockSpec((D, tn), lambda b, i, j: (0, j))],
        out_specs=pl.BlockSpec((1, tm, tn), lambda b, i, j: (b, i, j)),
        out_shape=jax.ShapeDtypeStruct((B, N, NP), F32),
        scratch_shapes=[pltpu.VMEM((tm, D), BF16)],
        compiler_params=pltpu.CompilerParams(
            dimension_semantics=("parallel", "parallel", "arbitrary"),
            vmem_limit_bytes=VMEM_LIMIT),
        name="in_proj",
    )(xa, modsel, nw.reshape(1, D), w)


def _merge_kernel(x_ref, yrw_ref, yret_ref, yhg_ref, g0_ref, g1_ref, g2_ref, wb_ref, wo_ref,
                  mod_ref, o_ref, *, tm, n_ctx, tile_off):
    i = pl.program_id(1) + tile_off
    merged = None
    for b, (y_ref, g_ref) in enumerate(((yrw_ref, g0_ref), (yret_ref, g1_ref), (yhg_ref, g2_ref))):
        z = _sigmoid(g_ref[0]) * _dot(y_ref[0], wb_ref[b])
        merged = z if merged is None else merged + z
    out = _dot(merged, wo_ref[...])
    g1 = jnp.where(_is_ctx_rows(i, tm, n_ctx), mod_ref[0, 2:3, :], mod_ref[0, 8:9, :])
    o_ref[0] = x_ref[0] + g1 * out


def _merge_call(xa, y_rw, y_ret, y_hg, P, wb, wo, modsel, n_ctx, latent_only, tm=256):
    B, N, D = xa.shape
    W = y_rw.shape[-1]
    off = n_ctx // tm if latent_only else 0
    n_rows = N - off * tm
    row = lambda b, i: (b, i + off, 0)
    return pl.pallas_call(
        functools.partial(_merge_kernel, tm=tm, n_ctx=n_ctx, tile_off=off),
        grid=(B, n_rows // tm),
        in_specs=[pl.BlockSpec((1, tm, D), row),
                  pl.BlockSpec((1, tm, W), row),
                  pl.BlockSpec((1, tm, W), row),
                  pl.BlockSpec((1, tm, W), row),
                  pl.BlockSpec((1, tm, D), lambda b, i: (b, i + off, 0)),
                  pl.BlockSpec((1, tm, D), lambda b, i: (b, i + off, 1)),
                  pl.BlockSpec((1, tm, D), lambda b, i: (b, i + off, 2)),
                  pl.BlockSpec((3, W, D), lambda b, i: (0, 0, 0)),
                  pl.BlockSpec((D, D), lambda b, i: (0, 0)),
                  pl.BlockSpec((1, 12, D), lambda b, i: (b, 0, 0))],
        out_specs=pl.BlockSpec((1, tm, D), lambda b, i: (b, i, 0)),
        out_shape=jax.ShapeDtypeStruct((B, n_rows, D), F32),
        compiler_params=pltpu.CompilerParams(
            dimension_semantics=("parallel", "parallel"), vmem_limit_bytes=VMEM_LIMIT),
        name="merge_out",
    )(xa, y_rw, y_ret, y_hg, P, P, P, wb, wo, modsel)


def _mlp_kernel(x_ref, mod_ref, nw_ref, w1_ref, w2_ref, fw_ref, o_ref, *, tm, n_ctx, tile_off, final):
    i = pl.program_id(1) + tile_off
    is_ctx = _is_ctx_rows(i, tm, n_ctx)
    xf = x_ref[0]
    h = _modulated_norm(xf, nw_ref[...], mod_ref, is_ctx, 3)
    a = jnp.maximum(_dot(h, w1_ref[...]), 0.0)
    out = _dot(a * a, w2_ref[...])
    g2 = jnp.where(is_ctx, mod_ref[0, 5:6, :], mod_ref[0, 11:12, :])
    xn = xf + g2 * out
    if final:
        xn = xn * lax.rsqrt(jnp.mean(xn * xn, axis=-1, keepdims=True) + NORM_EPS) * fw_ref[...]
    o_ref[0] = xn


def _mlp_call(xa, modsel, nw, w1, w2, fw, n_ctx, tile_off, final, tm=256):
    B, n_rows, D = xa.shape
    DF = w1.shape[1]
    return pl.pallas_call(
        functools.partial(_mlp_kernel, tm=tm, n_ctx=n_ctx, tile_off=tile_off, final=final),
        grid=(B, n_rows // tm),
        in_specs=[pl.BlockSpec((1, tm, D), lambda b, i: (b, i, 0)),
                  pl.BlockSpec((1, 12, D), lambda b, i: (b, 0, 0)),
                  pl.BlockSpec((1, D), lambda b, i: (0, 0)),
                  pl.BlockSpec((D, DF), lambda b, i: (0, 0)),
                  pl.BlockSpec((DF, D), lambda b, i: (0, 0)),
                  pl.BlockSpec((1, D), lambda b, i: (0, 0))],
        out_specs=pl.BlockSpec((1, tm, D), lambda b, i: (b, i, 0)),
        out_shape=jax.ShapeDtypeStruct((B, n_rows, D), F32),
        compiler_params=pltpu.CompilerParams(
            dimension_semantics=("parallel", "parallel"), vmem_limit_bytes=VMEM_LIMIT),
        name="mlp",
    )(xa, modsel, nw.reshape(1, D), w1, w2, fw.reshape(1, D))


def _chunk_index(d, j, n_ctx_chunks, n_chunks):
    bwd = jnp.where(j < n_ctx_chunks, n_ctx_chunks - 1 - j, n_chunks - 1 + n_ctx_chunks - j)
    return jnp.where(d == 0, j, bwd)


def _order_masks(d, n):
    t = lax.broadcasted_iota(jnp.int32, (n, n), 0)
    s = lax.broadcasted_iota(jnp.int32, (n, n), 1)
    lead = (t - s) * (1 - 2 * d)
    return lead > 0, lead >= 0


def _scan_specs(C, W, n_ctx_chunks, n_chunks):
    cidx = functools.partial(_chunk_index, n_ctx_chunks=n_ctx_chunks, n_chunks=n_chunks)
    shared = pl.BlockSpec((1, C, W), lambda b, d, j: (b, cidx(d, j), 0))
    per_dir = pl.BlockSpec((1, 1, C, W), lambda b, d, j: (d, b, cidx(d, j), 0))
    return shared, per_dir


def _rwkv_kernel(r_ref, v_ref, kk_ref, lw_ref, k_ref, a_ref, o_ref, s_ref, *, C):
    d = pl.program_id(1)
    HG = 4 * RWKV_HEAD
    n_stack = 4 * C

    @pl.when(pl.program_id(2) == 0)
    def _():
        s_ref[...] = jnp.zeros_like(s_ref)

    _, incl_c = _order_masks(d, C)
    lw = lw_ref[0, 0]
    G = _cumsum_rows(incl_c.astype(F32), lw)
    g_end = jnp.sum(lw, axis=0, keepdims=True)
    e_pos = jnp.exp(G)
    e_neg = jnp.exp(-G)
    e_excl = jnp.exp(G - lw)
    e_end = jnp.exp(g_end)
    kk = kk_ref[0]
    p = -(a_ref[0, 0] * kk)
    q_t = kk * e_excl
    r_t = r_ref[0] * e_pos
    p_t = p * e_neg
    k_t = k_ref[0, 0] * e_neg
    p_hat = p_t * e_end
    k_hat = k_t * e_end
    v = v_ref[0]

    rb = lax.broadcasted_iota(jnp.int32, (n_stack, HG), 0)
    lb = lax.broadcasted_iota(jnp.int32, (n_stack, HG), 1)
    same_head = (rb // C) == (lb // RWKV_HEAD)
    tt = rb % C
    ss = lb % C
    lead = (tt - ss) * (1 - 2 * d)
    strict_bd = same_head & (lead > 0)
    incl_bd = same_head & (lead >= 0)
    eye = (rb == lb).astype(F32)
    lane_head = lax.broadcasted_iota(jnp.int32, (C, HG), 1) // RWKV_HEAD

    def tile4(x):
        return jnp.concatenate([x, x, x, x], axis=0)

    def unstack(x):
        out = jnp.zeros((C, HG), F32)
        for h in range(4):
            out = jnp.where(lane_head == h, x[h * C:(h + 1) * C, :], out)
        return out

    outs = []
    for g in range(BRANCH_W // HG):
        sl = slice(g * HG, (g + 1) * HG)
        qs = jnp.where(same_head, tile4(q_t[:, sl]), 0.0).astype(BF16)
        rs = jnp.where(same_head, tile4(r_t[:, sl]), 0.0).astype(BF16)
        pt = tile4(p_t[:, sl]).astype(BF16)
        kt = tile4(k_t[:, sl]).astype(BF16)
        vt = tile4(v[:, sl]).astype(BF16)
        a_qp = jnp.where(strict_bd, _dot_nt(qs, pt), 0.0)
        a_qk = jnp.where(strict_bd, _dot_nt(qs, kt), 0.0)
        a_rp = jnp.where(incl_bd, _dot_nt(rs, pt), 0.0)
        a_rk = jnp.where(incl_bd, _dot_nt(rs, kt), 0.0)

        pw = a_qp.astype(BF16)
        T = eye + a_qp
        span = 2
        while span < C:
            pw_f = _dot(pw, pw)
            pw = pw_f.astype(BF16)
            T = T + _dot(pw, T)
            span *= 2
        Tb = T.astype(BF16)

        S = s_ref[g]
        Sb = S.astype(BF16)
        av = _dot(a_qk, vt)
        w_s = _dot(Tb, qs)
        u0_s = jnp.where(same_head, _dot(Tb, av), 0.0)
        u_s = _dot_nt(w_s, Sb) + u0_s
        o_s = _dot(a_rp, u_s) + _dot(a_rk, vt)
        outs.append(unstack(o_s) + _dot_nt(r_t[:, sl], Sb))
        u_n = unstack(u_s)
        upd = _dot_tn(jnp.concatenate([u_n, v[:, sl]], axis=0),
                      jnp.concatenate([p_hat[:, sl], k_hat[:, sl]], axis=0))
        s_ref[g] = S * e_end[:, sl] + jnp.where(same_head, upd, 0.0)

    o_ref[0, 0] = jnp.concatenate(outs, axis=1)


def _rwkv_call(r, v, kk, lw, k, a, n_ctx):
    B, N, W = r.shape
    C = RWKV_CHUNK
    n_chunks = N // C
    shared, per_dir = _scan_specs(C, W, n_ctx // C, n_chunks)
    return pl.pallas_call(
        functools.partial(_rwkv_kernel, C=C),
        grid=(B, 2, n_chunks),
        in_specs=[shared, shared, shared, per_dir, per_dir, per_dir],
        out_specs=per_dir,
        out_shape=jax.ShapeDtypeStruct((2, B, N, W), F32),
        scratch_shapes=[pltpu.VMEM((W // (4 * RWKV_HEAD), 4 * RWKV_HEAD, 4 * RWKV_HEAD), F32)],
        compiler_params=pltpu.CompilerParams(
            dimension_semantics=("parallel", "parallel", "arbitrary"), vmem_limit_bytes=VMEM_LIMIT),
        name="rwkv7_scan",
    )(r, v, kk, lw, k, a)


def _ret_kernel(q_ref, k_ref, v_ref, lg_ref, o_ref, s_ref, *, C):
    d = pl.program_id(1)
    Dh = RET_HEAD

    @pl.when(pl.program_id(2) == 0)
    def _():
        s_ref[...] = jnp.zeros_like(s_ref)

    _, incl = _order_masks(d, C)
    t = lax.broadcasted_iota(jnp.int32, (C, C), 0)
    s = lax.broadcasted_iota(jnp.int32, (C, C), 1)
    dist = jnp.abs(t - s).astype(F32)
    tcol = lax.broadcasted_iota(jnp.int32, (C, 1), 0)
    n_t = jnp.where(d == 0, tcol + 1, C - tcol).astype(F32)
    outs = []
    for h in range(BRANCH_W // Dh):
        sl = slice(h * Dh, (h + 1) * Dh)
        lg = lg_ref[0, :, h * Dh:h * Dh + C]
        qh, kh, vh = q_ref[0, :, sl], k_ref[0, :, sl], v_ref[0, :, sl]
        A = _dot_nt(qh, kh) * jnp.where(incl, jnp.exp(lg * dist), 0.0)
        lgd = lg_ref[0, :, sl]
        H = s_ref[h]
        outs.append(_dot(A, vh) + _dot(qh * jnp.exp(lgd * n_t), H))
        k_hat = kh * jnp.exp(lgd * (C - n_t))
        s_ref[h] = H * jnp.exp(lgd * C) + _dot_tn(k_hat, vh)
    o_ref[0, 0] = jnp.concatenate(outs, axis=1)


def _ret_call(q, k, v, lg, n_ctx):
    B, N, W = q.shape
    C = RET_CHUNK
    n_chunks = N // C
    shared, per_dir = _scan_specs(C, W, n_ctx // C, n_chunks)
    return pl.pallas_call(
        functools.partial(_ret_kernel, C=C),
        grid=(B, 2, n_chunks),
        in_specs=[shared, shared, shared, pl.BlockSpec((1, 1, W), lambda b, d, j: (d, 0, 0))],
        out_specs=per_dir,
        out_shape=jax.ShapeDtypeStruct((2, B, N, W), F32),
        scratch_shapes=[pltpu.VMEM((W // RET_HEAD, RET_HEAD, RET_HEAD), F32)],
        compiler_params=pltpu.CompilerParams(
            dimension_semantics=("parallel", "parallel", "arbitrary"), vmem_limit_bytes=VMEM_LIMIT),
        name="retention_scan",
    )(q, k, v, lg)


def _hgrn_kernel(q_ref, v_ref, k_ref, lf_ref, o_ref, s_ref, *, C, SUB):
    d = pl.program_id(1)
    Dh = HGRN_HEAD
    n_sub = C // SUB

    @pl.when(pl.program_id(2) == 0)
    def _():
        s_ref[...] = jnp.zeros_like(s_ref)

    _, incl = _order_masks(d, C)
    lf = lf_ref[0, 0]
    G = _cumsum_rows(incl.astype(F32), lf)
    g_end = jnp.sum(lf, axis=0, keepdims=True)
    q = q_ref[0]
    k = k_ref[0, 0]
    v = v_ref[0]
    W = q.shape[-1]
    anchors = [G[I * SUB + SUB // 2:I * SUB + SUB // 2 + 1, :] for I in range(n_sub)]
    g_anchor = jnp.concatenate([jnp.broadcast_to(a, (SUB, W)) for a in anchors], axis=0)
    q_a = q * jnp.exp(G - g_anchor)
    q_full = q * jnp.exp(G)
    k_hat = k * jnp.exp(g_end - G)
    sub_of_row = lax.broadcasted_iota(jnp.int32, (C, 1), 0) // SUB
    k_anch = []
    for I in range(n_sub):
        visible = (I - sub_of_row) * (1 - 2 * d) >= 0
        k_anch.append(k * jnp.exp(jnp.where(visible, anchors[I] - G, 0.0)))
    e_end = jnp.exp(g_end)
    outs = []
    for h in range(W // Dh):
        sl = slice(h * Dh, (h + 1) * Dh)
        A = jnp.concatenate([_dot_nt(q_a[I * SUB:(I + 1) * SUB, sl], k_anch[I][:, sl]) for I in range(n_sub)],
                            axis=0)
        A = jnp.where(incl, A, 0.0)
        S = s_ref[h]
        outs.append(_dot(A, v[:, sl]) + _dot_nt(q_full[:, sl], S))
        s_ref[h] = S * e_end[:, sl] + _dot_tn(v[:, sl], k_hat[:, sl])
    o_ref[0, 0] = jnp.concatenate(outs, axis=1)


def _hgrn_call(q, v, k, lf, n_ctx):
    B, N, W = q.shape
    C = HGRN_CHUNK
    n_chunks = N // C
    shared, per_dir = _scan_specs(C, W, n_ctx // C, n_chunks)
    return pl.pallas_call(
        functools.partial(_hgrn_kernel, C=C, SUB=HGRN_SUB),
        grid=(B, 2, n_chunks),
        in_specs=[shared, shared, per_dir, per_dir],
        out_specs=per_dir,
        out_shape=jax.ShapeDtypeStruct((2, B, N, W), F32),
        scratch_shapes=[pltpu.VMEM((W // HGRN_HEAD, HGRN_HEAD, HGRN_HEAD), F32)],
        compiler_params=pltpu.CompilerParams(
            dimension_semantics=("parallel", "parallel", "arbitrary"), vmem_limit_bytes=VMEM_LIMIT),
        name="hgrn2_scan",
    )(q, v, k, lf)


def _nbr_mean_seq(p):
    pp = jnp.pad(p, ((0, 0), (1, 1), (0, 0)))
    return 0.5 * (pp[:, :-2] + pp[:, 2:])


def _nbr_mean_grid(p):
    B, T, Cc = p.shape
    g = jnp.pad(p.reshape(B, T // GRID_W, GRID_W, Cc), ((0, 0), (1, 1), (1, 1), (0, 0)))
    m = 0.25 * (g[:, :-2, 1:-1] + g[:, 2:, 1:-1] + g[:, 1:-1, :-2] + g[:, 1:-1, 2:])
    return m.reshape(B, T, Cc)


def _per_head(x, dh):
    return x.reshape(x.shape[:-1] + (x.shape[-1] // dh, dh))


def _rope_tables(n, half):
    inv_freq = ROPE_BASE ** (-jnp.arange(half, dtype=F32) / half)
    ang = jnp.arange(n).astype(F32)[:, None] * inv_freq[None, :]
    return jnp.cos(ang), jnp.sin(ang)


def _rope(x, cos, sin, dh):
    xh = _per_head(x, dh)
    half = dh // 2
    x1, x2 = xh[..., :half], xh[..., half:]
    c, s = cos[:, None, :], sin[:, None, :]
    return jnp.concatenate([x1 * c - x2 * s, x1 * s + x2 * c], axis=-1).reshape(x.shape)


def _silu(x):
    return x * jax.nn.sigmoid(x)


def _permute_w_in(w_in_l, v_down_l):
    D = w_in_l.shape[0]
    n_rw, n_ret, n_hg = 3 * BRANCH_W + LORA_W, 4 * BRANCH_W, 5 * BRANCH_W
    rw = w_in_l[:, :n_rw]
    ret = w_in_l[:, n_rw:n_rw + n_ret]
    hg = w_in_l[:, n_rw + n_ret:n_rw + n_ret + n_hg]
    gate = w_in_l[:, n_rw + n_ret + n_hg:]
    extra = jnp.zeros((D, 128), F32)
    if v_down_l is not None:
        extra = extra.at[:, :v_down_l.shape[1]].set(v_down_l)
    return jnp.concatenate([gate, rw, extra, ret, hg], axis=1).astype(BF16)


def kernel(x, c, ctx, c_ctx, ada_w, ada_b, norm1_w, norm2_w, w_in, rwkv_mu, rwkv_w0, rwkv_w_up, rwkv_a0, rwkv_a_up, rwkv_g_up, rwkv_k_k, rwkv_k_a, rwkv_r_k, rwkv_lnx_w, rwkv_lnx_b, rwkv_v0, rwkv_v_down, rwkv_v_up, ret_decay, hgrn_lb, hgrn_norm_w, w_branch, w_out, mlp_w1, mlp_w2, final_norm_w):
    B, SEQ, D = x.shape
    n_ctx = ctx.shape[1]
    N = n_ctx + SEQ
    L = ada_w.shape[0]
    BW = BRANCH_W
    o_gate, o_rw = 0, 3 * D
    o_lora = o_rw + 3 * BW
    o_vd = o_lora + LORA_W
    o_ret = o_vd + 128
    o_hg = o_ret + 4 * BW

    xa = jnp.concatenate([ctx, x], axis=1)
    cvec = jnp.concatenate([c, c_ctx[None], jnp.zeros((8 - B - 1, D), F32)], axis=0)
    mod = _ada_call(cvec, ada_w, ada_b)
    sm = jax.nn.softmax(hgrn_lb.astype(F32), axis=0)
    hgrn_lower = jnp.cumsum(sm, axis=0) - sm[0:1]
    cos, sin = _rope_tables(N, RET_HEAD // 2)
    v_first = None
    out = None

    for l in range(L):
        last = l == L - 1
        mod_c = jnp.broadcast_to(mod[l, B].reshape(1, 6, D), (B, 6, D))
        modsel = jnp.concatenate([mod_c, mod[l, :B].reshape(B, 6, D)], axis=1)
        w_l = _permute_w_in(w_in[l], None if l == 0 else rwkv_v_down[l - 1])
        P = _proj_call(xa, modsel, norm1_w[l], w_l, n_ctx)

        p = jnp.concatenate([P[..., o_rw:o_rw + 3 * BW], P[..., o_lora:o_lora + LORA_W]], axis=-1)
        shifted = jnp.concatenate([_nbr_mean_seq(p[:, :n_ctx]), _nbr_mean_grid(p[:, n_ctx:])], axis=1)
        p = p + rwkv_mu[l] * (shifted - p)
        r, k, v = p[..., :BW], p[..., BW:2 * BW], p[..., 2 * BW:3 * BW]
        lo = p[..., 3 * BW:]
        wd = (lo[..., 0:64], lo[..., 64:128])
        ad = (lo[..., 128:192], lo[..., 192:256])
        gd = lo[..., 256:384]
        if l == 0:
            v_first = v
        else:
            hv = P[..., o_vd:o_vd + rwkv_v_down.shape[-1]]
            v = v + (v_first - v) * jax.nn.sigmoid(rwkv_v0[l - 1] + hv @ rwkv_v_up[l - 1])
        g = jax.nn.sigmoid(gd) @ rwkv_g_up[l]
        kk = _per_head(k * rwkv_k_k[l], RWKV_HEAD)
        kk = (kk * lax.rsqrt(jnp.sum(kk * kk, axis=-1, keepdims=True) + L2_EPS)).reshape(k.shape)
        lw_d, k_d, a_d = [], [], []
        for di in range(2):
            w_raw = -jax.nn.softplus(-(rwkv_w0[l, di] + jnp.tanh(wd[di]) @ rwkv_w_up[l, di])) - 0.5
            a = jax.nn.sigmoid(rwkv_a0[l, di] + ad[di] @ rwkv_a_up[l, di])
            lw_d.append(-jnp.exp(w_raw))
            a_d.append(a)
            k_d.append(k * (1.0 + (a - 1.0) * rwkv_k_a[l]))
        k_d = jnp.stack(k_d)
        o2 = _rwkv_call(r, v, kk, jnp.stack(lw_d), k_d, jnp.stack(a_d), n_ctx)
        o = _per_head(o2[0] + o2[1], RWKV_HEAD)
        mean = jnp.mean(o, axis=-1, keepdims=True)
        var = jnp.mean(jnp.square(o - mean), axis=-1, keepdims=True)
        o = ((o - mean) * lax.rsqrt(var + RWKV_LN_EPS)).reshape(r.shape) * rwkv_lnx_w[l] + rwkv_lnx_b[l]
        bonus = jnp.sum(_per_head(r * (k_d[0] + k_d[1]) * rwkv_r_k[l], RWKV_HEAD), axis=-1, keepdims=True)
        bonus = (bonus * _per_head(v, RWKV_HEAD)).reshape(r.shape)
        y_rw = (o + bonus) * g

        rq = _rope(P[..., o_ret:o_ret + BW], cos, sin, RET_HEAD)
        rk = _rope(P[..., o_ret + BW:o_ret + 2 * BW], cos, sin, RET_HEAD) * (RET_HEAD ** -0.5)
        rv = P[..., o_ret + 2 * BW:o_ret + 3 * BW]
        rg = P[..., o_ret + 3 * BW:o_ret + 4 * BW]
        log_gamma = -jnp.exp(ret_decay[l].astype(F32))
        lg = jnp.repeat(log_gamma, RET_HEAD, axis=-1).reshape(2, 1, BW)
        o2 = _ret_call(rq, rk, rv, lg, n_ctx)
        o = _per_head(o2[0] + o2[1], RET_HEAD)
        o = o * lax.rsqrt(jnp.mean(o * o, axis=-1, keepdims=True) + NORM_EPS)
        y_ret = o.reshape(rq.shape) * _silu(rg)

        hq = _silu(P[..., o_hg:o_hg + BW])
        hi = P[..., o_hg + 3 * BW:o_hg + 4 * BW]
        hgate = P[..., o_hg + 4 * BW:o_hg + 5 * BW]
        hk, hlf = [], []
        for di in range(2):
            f = P[..., o_hg + (1 + di) * BW:o_hg + (2 + di) * BW]
            lbd = hgrn_lower[l, di]
            hlf.append(jnp.log(lbd + (1.0 - lbd) * jax.nn.sigmoid(f)))
            hk.append((1.0 - lbd) * jax.nn.sigmoid(-f))
        o2 = _hgrn_call(hq, hi, jnp.stack(hk), jnp.stack(hlf), n_ctx)
        o = _per_head(o2[0] + o2[1], HGRN_HEAD)
        o = o * lax.rsqrt(jnp.mean(o * o, axis=-1, keepdims=True) + NORM_EPS)
        y_hg = o.reshape(hq.shape) * hgrn_norm_w[l] * _silu(hgate)

        xm = _merge_call(xa, y_rw, y_ret, y_hg, P, w_branch[l].astype(BF16), w_out[l].astype(BF16),
                         modsel, n_ctx, last)
        xa = _mlp_call(xm, modsel, norm2_w[l], mlp_w1[l].astype(BF16), mlp_w2[l].astype(BF16),
                       final_norm_w, n_ctx, (n_ctx // 256) if last else 0, last)
        out = xa
    return out
```

```python
import functools

import jax
import jax.numpy as jnp
from jax import lax
from jax.experimental import pallas as pl
from jax.experimental.pallas import tpu as pltpu

F32 = jnp.float32
BF16 = jnp.bfloat16

NORM_EPS = 1e-6
RWKV_LN_EPS = 64e-5
L2_EPS = 1e-12
ROPE_BASE = 10000.0
GRID_W = 64

BRANCH_W = 512
RWKV_HEAD = 64
RET_HEAD = 128
HGRN_HEAD = 128
DECAY_LORA = 64
AAA_LORA = 64
GATE_LORA = 128
LORA_W = 2 * DECAY_LORA + 2 * AAA_LORA + GATE_LORA
VDOWN_PAD = 128

RWKV_CHUNK = 64
HGRN_CHUNK = 64
HGRN_SUB = 16
RET_CHUNK = 128
ROW_TILE = 256

VMEM_LIMIT = 56 * 1024 * 1024


def _sigmoid(x):
    return 1.0 / (1.0 + jnp.exp(-x))


def _dot(a, b):
    return jnp.dot(a.astype(BF16), b.astype(BF16), preferred_element_type=F32)


def _dot_nt(a, b):
    return lax.dot_general(a.astype(BF16), b.astype(BF16), (((1,), (1,)), ((), ())),
                           preferred_element_type=F32)


def _dot_tn(a, b):
    return lax.dot_general(a.astype(BF16), b.astype(BF16), (((0,), (0,)), ((), ())),
                           preferred_element_type=F32)


def _cumsum_rows(mask_f32, x):
    return jnp.dot(mask_f32, x, precision=lax.Precision.HIGHEST, preferred_element_type=F32)


def _head_sum(x, ones_bd):
    hi = x.astype(BF16)
    lo = (x - hi.astype(F32)).astype(BF16)
    return (jnp.dot(hi, ones_bd, preferred_element_type=F32)
            + jnp.dot(lo, ones_bd, preferred_element_type=F32))


def _ada_kernel(c_ref, w_ref, b_ref, o_ref):
    cv = c_ref[...]
    s = cv * _sigmoid(cv)
    o_ref[0] = _dot(s, w_ref[0]) + b_ref[0]


def _ada_call(cvec, ada_w, ada_b):
    L, D, D6 = ada_w.shape
    tn = D6 // 4
    return pl.pallas_call(
        _ada_kernel,
        grid=(L, D6 // tn),
        in_specs=[pl.BlockSpec((8, D), lambda l, j: (0, 0)),
                  pl.BlockSpec((1, D, tn), lambda l, j: (l, 0, j)),
                  pl.BlockSpec((1, 1, tn), lambda l, j: (l, 0, j))],
        out_specs=pl.BlockSpec((1, 8, tn), lambda l, j: (l, 0, j)),
        out_shape=jax.ShapeDtypeStruct((L, 8, D6), F32),
        compiler_params=pltpu.CompilerParams(vmem_limit_bytes=VMEM_LIMIT),
        name="ada_mod",
    )(cvec, ada_w, ada_b.reshape(L, 1, D6))


def _modulated_norm(xf, nw, mod_ref, is_ctx, which):
    y = xf * lax.rsqrt(jnp.mean(xf * xf, axis=-1, keepdims=True) + NORM_EPS) * nw
    sh = jnp.where(is_ctx, mod_ref[0, which:which + 1, :], mod_ref[0, 6 + which:7 + which, :])
    sc = jnp.where(is_ctx, mod_ref[0, which + 1:which + 2, :], mod_ref[0, 7 + which:8 + which, :])
    return y * (1.0 + sc) + sh


def _is_ctx_rows(tile_idx, tm, n_ctx):
    row = tile_idx * tm + lax.broadcasted_iota(jnp.int32, (tm, 1), 0)
    return row < n_ctx


def _proj_kernel(x_ref, mod_ref, nw_ref, w_ref, o_ref, h_ref, *, tm, n_ctx):
    i = pl.program_id(1)

    @pl.when(pl.program_id(2) == 0)
    def _():
        h = _modulated_norm(x_ref[0], nw_ref[...], mod_ref, _is_ctx_rows(i, tm, n_ctx), 0)
        h_ref[...] = h.astype(BF16)

    o_ref[0] = jnp.dot(h_ref[...], w_ref[...], preferred_element_type=F32)


def _proj_call(xa, modsel, nw, w, n_ctx, tm=768, n_tiles=4):
    B, N, D = xa.shape
    NP = w.shape[1]
    tn = NP // n_tiles
    return pl.pallas_call(
        functools.partial(_proj_kernel, tm=tm, n_ctx=n_ctx),
        grid=(B, N // tm, n_tiles),
        in_specs=[pl.BlockSpec((1, tm, D), lambda b, i, j: (b, i, 0)),
                  pl.BlockSpec((1, 12, D), lambda b, i, j: (b, 0, 0)),
                  pl.BlockSpec((1, D), lambda b, i, j: (0, 0)),
                  pl.BlockSpec((D, tn), lambda b, i, j: (0, j))],
        out_specs=pl.BlockSpec((1, tm, tn), lambda b, i, j: (b, i, j)),
        out_shape=jax.ShapeDtypeStruct((B, N, NP), F32),
        scratch_shapes=[pltpu.VMEM((tm, D), BF16)],
        compiler_params=pltpu.CompilerParams(
            dimension_semantics=("parallel", "parallel", "arbitrary"),
            vmem_limit_bytes=VMEM_LIMIT),
        name="in_proj",
    )(xa, modsel, nw.reshape(1, D), w)


def _shift_mix(cur, prev, nxt, mu, i, seq):
    tm = cur.shape[0]
    row = lax.broadcasted_iota(jnp.int32, (tm, 1), 0)
    is_ctx = i == 0
    period_mask = jnp.where(is_ctx, tm - 1, GRID_W - 1)
    col = row & period_mask
    ext = jnp.concatenate([prev, cur, nxt], axis=0)
    up = ext[0:tm]
    down = ext[2 * GRID_W:2 * GRID_W + tm]
    lat_row = (i - 1) * tm + row
    up = jnp.where(lat_row >= GRID_W, up, 0.0)
    down = jnp.where(lat_row < seq - GRID_W, down, 0.0)
    left = jnp.where(col != 0, pltpu.roll(cur, 1, 0), 0.0)
    right = jnp.where(col != period_mask, pltpu.roll(cur, tm - 1, 0), 0.0)
    w_lr = jnp.where(is_ctx, 0.5, 0.25)
    w_ud = jnp.where(is_ctx, 0.0, 0.25)
    shifted = w_lr * (left + right) + w_ud * (up + down)
    return cur + mu * (shifted - cur)


def _prep_kernel(pc_ref, pp_ref, pn_ref, lc_ref, lp_ref, ln_ref, hv_ref, vf_ref, rqk_ref, cs_ref, sn_ref,
                 hq_ref, hf0_ref, hf1_ref,
                 mum_ref, mul_ref, wup_ref, w0_ref, aup_ref, a0_ref, gup_ref, kk_w_ref, ka_ref, rk_ref,
                 v0_ref, vup_ref, ones_ref, lb_ref,
                 r_o, v_o, kk_o, lw_o, kd_o, a_o, g_o, bonus_o, rq_o, rk_o, hq_o, hk_o, hlf_o,
                 *, seq, has_vres):
    i = pl.program_id(1)
    W = BRANCH_W

    p = _shift_mix(pc_ref[0], pp_ref[0], pn_ref[0], mum_ref[...], i, seq)
    lo = _shift_mix(lc_ref[0], lp_ref[0], ln_ref[0], mul_ref[...], i, seq)
    r, k, v = p[:, :W], p[:, W:2 * W], p[:, 2 * W:]
    if has_vres:
        mix = _sigmoid(v0_ref[...] + _dot(hv_ref[0], vup_ref[...]))
        v = v + (vf_ref[0] - v) * mix
    r_o[0] = r
    v_o[0] = v
    gd = lo[:, 2 * DECAY_LORA + 2 * AAA_LORA:]
    g_o[0] = _dot(_sigmoid(gd), gup_ref[...])
    kx = k * kk_w_ref[...]
    kk_o[0] = kx * lax.rsqrt(_head_sum(kx * kx, ones_ref[...]) + L2_EPS)
    k_sum = None
    for d in range(2):
        wd = lo[:, d * DECAY_LORA:(d + 1) * DECAY_LORA]
        ad = lo[:, 2 * DECAY_LORA + d * AAA_LORA:2 * DECAY_LORA + (d + 1) * AAA_LORA]
        z = -(w0_ref[d:d + 1, :] + _dot(jnp.tanh(wd), wup_ref[d]))
        softplus = jnp.maximum(z, 0.0) + jnp.log(1.0 + jnp.exp(-jnp.abs(z)))
        lw_o[d, 0] = -jnp.exp(-softplus - 0.5)
        a = _sigmoid(a0_ref[d:d + 1, :] + _dot(ad, aup_ref[d]))
        a_o[d, 0] = a
        kd = k * (1.0 + (a - 1.0) * ka_ref[...])
        kd_o[d, 0] = kd
        k_sum = kd if k_sum is None else k_sum + kd
    bonus_o[0] = _head_sum(r * k_sum * rk_ref[...], ones_ref[...]) * v

    cosf = jnp.concatenate([cs_ref[...]] * (W // RET_HEAD), axis=1)
    sinf = jnp.concatenate([sn_ref[...]] * (W // RET_HEAD), axis=1)
    lane = lax.broadcasted_iota(jnp.int32, (1, W), 1)
    first_half = (lane & (RET_HEAD - 1)) < RET_HEAD // 2

    def rope(t):
        rot = jnp.where(first_half, pltpu.roll(t, W - RET_HEAD // 2, 1), pltpu.roll(t, RET_HEAD // 2, 1))
        return t * cosf + rot * sinf

    rq_o[0] = rope(rqk_ref[0, :, :W])
    rk_o[0] = rope(rqk_ref[0, :, W:]) * (RET_HEAD ** -0.5)

    q = hq_ref[0]
    hq_o[0] = q * _sigmoid(q)
    for d, f_ref in enumerate((hf0_ref, hf1_ref)):
        f = f_ref[0]
        e = jnp.exp(-jnp.abs(f))
        inv = 1.0 / (1.0 + e)
        sig_pos = jnp.where(f >= 0, inv, e * inv)
        sig_neg = jnp.where(f >= 0, e * inv, inv)
        lb = lb_ref[d:d + 1, :]
        hlf_o[d, 0] = jnp.log(lb + (1.0 - lb) * sig_pos)
        hk_o[d, 0] = (1.0 - lb) * sig_neg


def _prep_call(P, cols, v_first, tables, wts, n_ctx, has_vres):
    B, N, _ = P.shape
    W = BRANCH_W
    tm = ROW_TILE
    seq = N - n_ctx
    halo_per_tile = tm // GRID_W
    n_halo = N // GRID_W
    c_main, c_lora, c_vd, c_rqk, c_hq = cols

    def cur(width, col):
        return pl.BlockSpec((1, tm, width), lambda b, i: (b, i, col))

    def prev(width, col):
        return pl.BlockSpec((1, GRID_W, width), lambda b, i: (b, jnp.maximum(i * halo_per_tile - 1, 0), col))

    def nxt(width, col):
        return pl.BlockSpec((1, GRID_W, width),
                            lambda b, i: (b, jnp.minimum((i + 1) * halo_per_tile, n_halo - 1), col))

    def const(shape):
        return pl.BlockSpec(shape, lambda b, i: (0,) * len(shape))

    tok = pl.BlockSpec((1, tm, W), lambda b, i: (b, i, 0))
    tok2 = pl.BlockSpec((2, 1, tm, W), lambda b, i: (0, b, i, 0))
    cos, sin = tables
    in_specs = [cur(3 * W, c_main), prev(3 * W, c_main), nxt(3 * W, c_main),
                cur(LORA_W, c_lora), prev(LORA_W, c_lora), nxt(LORA_W, c_lora),
                cur(VDOWN_PAD, c_vd), tok, cur(2 * W, c_rqk),
                pl.BlockSpec((tm, RET_HEAD), lambda b, i: (i, 0)),
                pl.BlockSpec((tm, RET_HEAD), lambda b, i: (i, 0)),
                cur(W, c_hq), cur(W, c_hq + 1), cur(W, c_hq + 2)]
    in_specs += [const(w.shape) for w in wts]
    one = jax.ShapeDtypeStruct((B, N, W), F32)
    two = jax.ShapeDtypeStruct((2, B, N, W), F32)
    out_shape = [one, one, one, two, two, two, one, one, one, one, one, two, two]
    out_specs = [tok, tok, tok, tok2, tok2, tok2, tok, tok, tok, tok, tok, tok2, tok2]
    return pl.pallas_call(
        functools.partial(_prep_kernel, seq=seq, has_vres=has_vres),
        grid=(B, N // tm),
        in_specs=in_specs,
        out_specs=out_specs,
        out_shape=out_shape,
        compiler_params=pltpu.CompilerParams(
            dimension_semantics=("parallel", "parallel"), vmem_limit_bytes=VMEM_LIMIT),
        name="branch_prep",
    )(P, P, P, P, P, P, P, v_first, P, cos, sin, P, P, P, *wts)


def _merge_kernel(x_ref, g0_ref, g1_ref, g2_ref, orw_ref, g_ref, bonus_ref, oret_ref, rg_ref, ohg_ref, hg_ref,
                  lnw_ref, lnb_ref, hnw_ref, m64_ref, m128_ref, wb_ref, wo_ref, mod_ref, o_ref,
                  *, tm, n_ctx, tile_off):
    i = pl.program_id(1) + tile_off

    o = orw_ref[0, 0] + orw_ref[1, 0]
    cen = o - _head_sum(o, m64_ref[...]) * (1.0 / RWKV_HEAD)
    var = _head_sum(cen * cen, m64_ref[...]) * (1.0 / RWKV_HEAD)
    y_rw = (cen * lax.rsqrt(var + RWKV_LN_EPS) * lnw_ref[...] + lnb_ref[...] + bonus_ref[0]) * g_ref[0]
    o = oret_ref[0, 0] + oret_ref[1, 0]
    ms = _head_sum(o * o, m128_ref[...]) * (1.0 / RET_HEAD)
    rg = rg_ref[0]
    y_ret = o * lax.rsqrt(ms + NORM_EPS) * (rg * _sigmoid(rg))
    o = ohg_ref[0, 0] + ohg_ref[1, 0]
    ms = _head_sum(o * o, m128_ref[...]) * (1.0 / HGRN_HEAD)
    hg = hg_ref[0]
    y_hg = o * lax.rsqrt(ms + NORM_EPS) * hnw_ref[...] * (hg * _sigmoid(hg))

    merged = None
    for b, (y, gate_ref) in enumerate(((y_rw, g0_ref), (y_ret, g1_ref), (y_hg, g2_ref))):
        z = _sigmoid(gate_ref[0]) * _dot(y, wb_ref[b])
        merged = z if merged is None else merged + z
    out = _dot(merged, wo_ref[...])
    g1 = jnp.where(_is_ctx_rows(i, tm, n_ctx), mod_ref[0, 2:3, :], mod_ref[0, 8:9, :])
    o_ref[0] = x_ref[0] + g1 * out


def _merge_call(xa, P, gate_cols, o_rw, g, bonus, o_ret, o_hg, wts, modsel, n_ctx, latent_only):
    B, N, D = xa.shape
    W = BRANCH_W
    tm = ROW_TILE
    off = n_ctx // tm if latent_only else 0
    n_rows = N - off * tm
    c_gate, c_rg, c_hg = gate_cols

    def col(width, c):
        return pl.BlockSpec((1, tm, width), lambda b, i: (b, i + off, c))

    both = pl.BlockSpec((2, 1, tm, W), lambda b, i: (0, b, i + off, 0))

    def const(shape):
        return pl.BlockSpec(shape, lambda b, i: (0,) * len(shape))

    return pl.pallas_call(
        functools.partial(_merge_kernel, tm=tm, n_ctx=n_ctx, tile_off=off),
        grid=(B, n_rows // tm),
        in_specs=[col(D, 0), col(D, c_gate), col(D, c_gate + 1), col(D, c_gate + 2),
                  both, col(W, 0), col(W, 0), both, col(W, c_rg), both, col(W, c_hg)]
                 + [const(w.shape) for w in wts]
                 + [pl.BlockSpec((1, 12, D), lambda b, i: (b, 0, 0))],
        out_specs=pl.BlockSpec((1, tm, D), lambda b, i: (b, i, 0)),
        out_shape=jax.ShapeDtypeStruct((B, n_rows, D), F32),
        compiler_params=pltpu.CompilerParams(
            dimension_semantics=("parallel", "parallel"), vmem_limit_bytes=VMEM_LIMIT),
        name="merge_out",
    )(xa, P, P, P, o_rw, g, bonus, o_ret, P, o_hg, P, *wts, modsel)


def _mlp_kernel(x_ref, mod_ref, nw_ref, w1_ref, w2_ref, fw_ref, o_ref, *, tm, n_ctx, tile_off, final):
    i = pl.program_id(1) + tile_off
    is_ctx = _is_ctx_rows(i, tm, n_ctx)
    xf = x_ref[0]
    h = _modulated_norm(xf, nw_ref[...], mod_ref, is_ctx, 3)
    a = jnp.maximum(_dot(h, w1_ref[...]), 0.0)
    out = _dot(a * a, w2_ref[...])
    g2 = jnp.where(is_ctx, mod_ref[0, 5:6, :], mod_ref[0, 11:12, :])
    xn = xf + g2 * out
    if final:
        xn = xn * lax.rsqrt(jnp.mean(xn * xn, axis=-1, keepdims=True) + NORM_EPS) * fw_ref[...]
    o_ref[0] = xn


def _mlp_call(xa, modsel, nw, w1, w2, fw, n_ctx, tile_off, final):
    B, n_rows, D = xa.shape
    DF = w1.shape[1]
    tm = ROW_TILE
    return pl.pallas_call(
        functools.partial(_mlp_kernel, tm=tm, n_ctx=n_ctx, tile_off=tile_off, final=final),
        grid=(B, n_rows // tm),
        in_specs=[pl.BlockSpec((1, tm, D), lambda b, i: (b, i, 0)),
                  pl.BlockSpec((1, 12, D), lambda b, i: (b, 0, 0)),
                  pl.BlockSpec((1, D), lambda b, i: (0, 0)),
                  pl.BlockSpec((D, DF), lambda b, i: (0, 0)),
                  pl.BlockSpec((DF, D), lambda b, i: (0, 0)),
                  pl.BlockSpec((1, D), lambda b, i: (0, 0))],
        out_specs=pl.BlockSpec((1, tm, D), lambda b, i: (b, i, 0)),
        out_shape=jax.ShapeDtypeStruct((B, n_rows, D), F32),
        compiler_params=pltpu.CompilerParams(
            dimension_semantics=("parallel", "parallel"), vmem_limit_bytes=VMEM_LIMIT),
        name="mlp",
    )(xa, modsel, nw.reshape(1, D), w1, w2, fw.reshape(1, D))


def _chunk_index(d, j, n_ctx_chunks, n_chunks):
    bwd = jnp.where(j < n_ctx_chunks, n_ctx_chunks - 1 - j, n_chunks - 1 + n_ctx_chunks - j)
    return jnp.where(d == 0, j, bwd)


def _order_masks(d, n):
    t = lax.broadcasted_iota(jnp.int32, (n, n), 0)
    s = lax.broadcasted_iota(jnp.int32, (n, n), 1)
    lead = (t - s) * (1 - 2 * d)
    return lead > 0, lead >= 0


def _scan_specs(C, W, n_ctx_chunks, n_chunks):
    cidx = functools.partial(_chunk_index, n_ctx_chunks=n_ctx_chunks, n_chunks=n_chunks)

    def shared(col=0):
        return pl.BlockSpec((1, C, W), lambda b, d, j: (b, cidx(d, j), col))

    per_dir = pl.BlockSpec((1, 1, C, W), lambda b, d, j: (d, b, cidx(d, j), 0))
    return shared, per_dir


def _rwkv_kernel(r_ref, v_ref, kk_ref, lw_ref, k_ref, a_ref, o_ref, s_ref, *, C):
    d = pl.program_id(1)
    HG = 4 * RWKV_HEAD
    n_stack = 4 * C

    @pl.when(pl.program_id(2) == 0)
    def _():
        s_ref[...] = jnp.zeros_like(s_ref)

    _, incl_c = _order_masks(d, C)
    lw = lw_ref[0, 0]
    G = _cumsum_rows(incl_c.astype(F32), lw)
    g_end = jnp.sum(lw, axis=0, keepdims=True)
    e_pos = jnp.exp(G)
    e_neg = jnp.exp(-G)
    e_excl = jnp.exp(G - lw)
    e_end = jnp.exp(g_end)
    kk = kk_ref[0]
    p = -(a_ref[0, 0] * kk)
    q_t = kk * e_excl
    r_t = r_ref[0] * e_pos
    p_t = p * e_neg
    k_t = k_ref[0, 0] * e_neg
    p_hat = p_t * e_end
    k_hat = k_t * e_end
    v = v_ref[0]

    rb = lax.broadcasted_iota(jnp.int32, (n_stack, HG), 0)
    lb = lax.broadcasted_iota(jnp.int32, (n_stack, HG), 1)
    same_head = (rb // C) == (lb // RWKV_HEAD)
    tt = rb % C
    ss = lb % C
    lead = (tt - ss) * (1 - 2 * d)
    strict_bd = same_head & (lead > 0)
    incl_bd = same_head & (lead >= 0)
    eye = (rb == lb).astype(F32)
    lane_head = lax.broadcasted_iota(jnp.int32, (C, HG), 1) // RWKV_HEAD

    def tile4(x):
        return jnp.concatenate([x, x, x, x], axis=0)

    def unstack(x):
        out = jnp.zeros((C, HG), F32)
        for h in range(4):
            out = jnp.where(lane_head == h, x[h * C:(h + 1) * C, :], out)
        return out

    outs = []
    for g in range(BRANCH_W // HG):
        sl = slice(g * HG, (g + 1) * HG)
        qs = jnp.where(same_head, tile4(q_t[:, sl]), 0.0).astype(BF16)
        rs = jnp.where(same_head, tile4(r_t[:, sl]), 0.0).astype(BF16)
        pt = tile4(p_t[:, sl]).astype(BF16)
        kt = tile4(k_t[:, sl]).astype(BF16)
        vt = tile4(v[:, sl]).astype(BF16)
        a_qp = jnp.where(strict_bd, _dot_nt(qs, pt), 0.0)
        a_qk = jnp.where(strict_bd, _dot_nt(qs, kt), 0.0)
        a_rp = jnp.where(incl_bd, _dot_nt(rs, pt), 0.0)
        a_rk = jnp.where(incl_bd, _dot_nt(rs, kt), 0.0)

        pw = a_qp.astype(BF16)
        T = eye + a_qp
        span = 2
        while span < C:
            pw_f = _dot(pw, pw)
            pw = pw_f.astype(BF16)
            T = T + _dot(pw, T)
            span *= 2
        Tb = T.astype(BF16)

        S = s_ref[g]
        Sb = S.astype(BF16)
        av = _dot(a_qk, vt)
        w_s = _dot(Tb, qs)
        u0_s = jnp.where(same_head, _dot(Tb, av), 0.0)
        u_s = _dot_nt(w_s, Sb) + u0_s
        o_s = _dot(a_rp, u_s) + _dot(a_rk, vt)
        outs.append(unstack(o_s) + _dot_nt(r_t[:, sl], Sb))
        u_n = unstack(u_s)
        upd = _dot_tn(jnp.concatenate([u_n, v[:, sl]], axis=0),
                      jnp.concatenate([p_hat[:, sl], k_hat[:, sl]], axis=0))
        s_ref[g] = S * e_end[:, sl] + jnp.where(same_head, upd, 0.0)

    o_ref[0, 0] = jnp.concatenate(outs, axis=1)


def _rwkv_call(r, v, kk, lw, k, a, n_ctx):
    B, N, W = r.shape
    C = RWKV_CHUNK
    assert C == RWKV_HEAD and W % (4 * RWKV_HEAD) == 0
    n_chunks = N // C
    shared, per_dir = _scan_specs(C, W, n_ctx // C, n_chunks)
    return pl.pallas_call(
        functools.partial(_rwkv_kernel, C=C),
        grid=(B, 2, n_chunks),
        in_specs=[shared(), shared(), shared(), per_dir, per_dir, per_dir],
        out_specs=per_dir,
        out_shape=jax.ShapeDtypeStruct((2, B, N, W), F32),
        scratch_shapes=[pltpu.VMEM((W // (4 * RWKV_HEAD), 4 * RWKV_HEAD, 4 * RWKV_HEAD), F32)],
        compiler_params=pltpu.CompilerParams(
            dimension_semantics=("parallel", "parallel", "arbitrary"), vmem_limit_bytes=VMEM_LIMIT),
        name="rwkv7_scan",
    )(r, v, kk, lw, k, a)


def _ret_kernel(q_ref, k_ref, v_ref, lg_ref, o_ref, s_ref, *, C):
    d = pl.program_id(1)
    Dh = RET_HEAD

    @pl.when(pl.program_id(2) == 0)
    def _():
        s_ref[...] = jnp.zeros_like(s_ref)

    _, incl = _order_masks(d, C)
    t = lax.broadcasted_iota(jnp.int32, (C, C), 0)
    s = lax.broadcasted_iota(jnp.int32, (C, C), 1)
    dist = jnp.abs(t - s).astype(F32)
    tcol = lax.broadcasted_iota(jnp.int32, (C, 1), 0)
    n_t = jnp.where(d == 0, tcol + 1, C - tcol).astype(F32)
    outs = []
    for h in range(BRANCH_W // Dh):
        sl = slice(h * Dh, (h + 1) * Dh)
        lg = lg_ref[0, :, h * Dh:h * Dh + C]
        qh, kh, vh = q_ref[0, :, sl], k_ref[0, :, sl], v_ref[0, :, sl]
        A = _dot_nt(qh, kh) * jnp.where(incl, jnp.exp(lg * dist), 0.0)
        lgd = lg_ref[0, :, sl]
        H = s_ref[h]
        outs.append(_dot(A, vh) + _dot(qh * jnp.exp(lgd * n_t), H))
        k_hat = kh * jnp.exp(lgd * (C - n_t))
        s_ref[h] = H * jnp.exp(lgd * C) + _dot_tn(k_hat, vh)
    o_ref[0, 0] = jnp.concatenate(outs, axis=1)


def _ret_call(q, k, v, v_col, lg, n_ctx):
    B, N, W = q.shape
    C = RET_CHUNK
    assert C <= RET_HEAD
    n_chunks = N // C
    shared, per_dir = _scan_specs(C, W, n_ctx // C, n_chunks)
    return pl.pallas_call(
        functools.partial(_ret_kernel, C=C),
        grid=(B, 2, n_chunks),
        in_specs=[shared(), shared(), shared(v_col), pl.BlockSpec((1, 1, W), lambda b, d, j: (d, 0, 0))],
        out_specs=per_dir,
        out_shape=jax.ShapeDtypeStruct((2, B, N, W), F32),
        scratch_shapes=[pltpu.VMEM((W // RET_HEAD, RET_HEAD, RET_HEAD), F32)],
        compiler_params=pltpu.CompilerParams(
            dimension_semantics=("parallel", "parallel", "arbitrary"), vmem_limit_bytes=VMEM_LIMIT),
        name="retention_scan",
    )(q, k, v, lg)


def _hgrn_kernel(q_ref, v_ref, k_ref, lf_ref, o_ref, s_ref, *, C, SUB):
    d = pl.program_id(1)
    Dh = HGRN_HEAD
    n_sub = C // SUB

    @pl.when(pl.program_id(2) == 0)
    def _():
        s_ref[...] = jnp.zeros_like(s_ref)

    _, incl = _order_masks(d, C)
    lf = lf_ref[0, 0]
    G = _cumsum_rows(incl.astype(F32), lf)
    g_end = jnp.sum(lf, axis=0, keepdims=True)
    q = q_ref[0]
    k = k_ref[0, 0]
    v = v_ref[0]
    W = q.shape[-1]
    anchors = [G[I * SUB + SUB // 2:I * SUB + SUB // 2 + 1, :] for I in range(n_sub)]
    g_anchor = jnp.concatenate([jnp.broadcast_to(a, (SUB, W)) for a in anchors], axis=0)
    q_a = q * jnp.exp(G - g_anchor)
    q_full = q * jnp.exp(G)
    k_hat = k * jnp.exp(g_end - G)
    sub_of_row = lax.broadcasted_iota(jnp.int32, (C, 1), 0) // SUB
    k_anch = []
    for I in range(n_sub):
        visible = (I - sub_of_row) * (1 - 2 * d) >= 0
        k_anch.append(k * jnp.exp(jnp.where(visible, anchors[I] - G, 0.0)))
    e_end = jnp.exp(g_end)
    outs = []
    for h in range(W // Dh):
        sl = slice(h * Dh, (h + 1) * Dh)
        A = jnp.concatenate([_dot_nt(q_a[I * SUB:(I + 1) * SUB, sl], k_anch[I][:, sl]) for I in range(n_sub)],
                            axis=0)
        A = jnp.where(incl, A, 0.0)
        S = s_ref[h]
        outs.append(_dot(A, v[:, sl]) + _dot_nt(q_full[:, sl], S))
        s_ref[h] = S * e_end[:, sl] + _dot_tn(v[:, sl], k_hat[:, sl])
    o_ref[0, 0] = jnp.concatenate(outs, axis=1)


def _hgrn_call(q, v, v_col, k, lf, n_ctx):
    B, N, W = q.shape
    C = HGRN_CHUNK
    n_chunks = N // C
    shared, per_dir = _scan_specs(C, W, n_ctx // C, n_chunks)
    return pl.pallas_call(
        functools.partial(_hgrn_kernel, C=C, SUB=HGRN_SUB),
        grid=(B, 2, n_chunks),
        in_specs=[shared(), shared(v_col), per_dir, per_dir],
        out_specs=per_dir,
        out_shape=jax.ShapeDtypeStruct((2, B, N, W), F32),
        scratch_shapes=[pltpu.VMEM((W // HGRN_HEAD, HGRN_HEAD, HGRN_HEAD), F32)],
        compiler_params=pltpu.CompilerParams(
            dimension_semantics=("parallel", "parallel", "arbitrary"), vmem_limit_bytes=VMEM_LIMIT),
        name="hgrn2_scan",
    )(q, v, k, lf)


def _rope_tables(n, head):
    half = head // 2
    inv_freq = ROPE_BASE ** (-jnp.arange(half, dtype=F32) / half)
    ang = jnp.arange(n).astype(F32)[:, None] * inv_freq[None, :]
    cos, sin = jnp.cos(ang), jnp.sin(ang)
    return jnp.concatenate([cos, cos], axis=1), jnp.concatenate([-sin, sin], axis=1)


def _block_diag_ones(width, head):
    idx = jnp.arange(width) // head
    return (idx[:, None] == idx[None, :]).astype(BF16)


def _permute_w_in(w_in_l, v_down_l):
    D = w_in_l.shape[0]
    n_rw, n_ret, n_hg = 3 * BRANCH_W + LORA_W, 4 * BRANCH_W, 5 * BRANCH_W
    rw = w_in_l[:, :n_rw]
    ret = w_in_l[:, n_rw:n_rw + n_ret]
    hg = w_in_l[:, n_rw + n_ret:n_rw + n_ret + n_hg]
    gate = w_in_l[:, n_rw + n_ret + n_hg:]
    extra = jnp.zeros((D, VDOWN_PAD), F32)
    if v_down_l is not None:
        extra = extra.at[:, :v_down_l.shape[1]].set(v_down_l)
    return jnp.concatenate([gate, rw, extra, ret, hg], axis=1).astype(BF16)


def kernel(x, c, ctx, c_ctx, ada_w, ada_b, norm1_w, norm2_w, w_in, rwkv_mu, rwkv_w0, rwkv_w_up, rwkv_a0, rwkv_a_up, rwkv_g_up, rwkv_k_k, rwkv_k_a, rwkv_r_k, rwkv_lnx_w, rwkv_lnx_b, rwkv_v0, rwkv_v_down, rwkv_v_up, ret_decay, hgrn_lb, hgrn_norm_w, w_branch, w_out, mlp_w1, mlp_w2, final_norm_w):
    B, SEQ, D = x.shape
    n_ctx = ctx.shape[1]
    N = n_ctx + SEQ
    L = ada_w.shape[0]
    W = BRANCH_W
    assert n_ctx == ROW_TILE and SEQ % ROW_TILE == 0 and D == 2 * W
    o_rw = 3 * D
    o_lora = o_rw + 3 * W
    o_vd = o_lora + LORA_W
    o_ret = o_vd + VDOWN_PAD
    o_hg = o_ret + 4 * W
    prep_cols = (o_rw // (3 * W), o_lora // LORA_W, o_vd // VDOWN_PAD, o_ret // (2 * W), o_hg // W)
    gate_cols = (0, (o_ret + 3 * W) // W, (o_hg + 4 * W) // W)
    assert o_rw % (3 * W) == 0 and o_lora % LORA_W == 0 and o_ret % (2 * W) == 0

    xa = jnp.concatenate([ctx, x], axis=1)
    cvec = jnp.concatenate([c, c_ctx[None], jnp.zeros((8 - B - 1, D), F32)], axis=0)
    mod = _ada_call(cvec, ada_w, ada_b)
    sm = jax.nn.softmax(hgrn_lb.astype(F32), axis=0)
    hgrn_lower = jnp.cumsum(sm, axis=0) - sm[0:1]
    tables = _rope_tables(N, RET_HEAD)
    ones64 = _block_diag_ones(W, RWKV_HEAD)
    ones128 = _block_diag_ones(W, RET_HEAD)
    row = lambda t: t.reshape(1, -1)
    v_first = None
    out = None

    for l in range(L):
        last = l == L - 1
        mod_c = jnp.broadcast_to(mod[l, B].reshape(1, 6, D), (B, 6, D))
        modsel = jnp.concatenate([mod_c, mod[l, :B].reshape(B, 6, D)], axis=1)
        w_l = _permute_w_in(w_in[l], None if l == 0 else rwkv_v_down[l - 1])
        P = _proj_call(xa, modsel, norm1_w[l], w_l, n_ctx)

        has_vres = l > 0
        if has_vres:
            v0 = row(rwkv_v0[l - 1])
            v_up = jnp.zeros((VDOWN_PAD, W), F32).at[:rwkv_v_up.shape[1]].set(rwkv_v_up[l - 1]).astype(BF16)
        else:
            v0 = jnp.zeros((1, W), F32)
            v_up = jnp.zeros((VDOWN_PAD, W), BF16)
        prep_w = (row(rwkv_mu[l, :3 * W]), row(rwkv_mu[l, 3 * W:]),
                  rwkv_w_up[l].astype(BF16), rwkv_w0[l], rwkv_a_up[l].astype(BF16), rwkv_a0[l],
                  rwkv_g_up[l].astype(BF16), row(rwkv_k_k[l]), row(rwkv_k_a[l]), row(rwkv_r_k[l]),
                  v0, v_up, ones64, hgrn_lower[l])
        (r, v, kk, lw, kd, a, g, bonus, rq, rk, hq, hk, hlf) = _prep_call(
            P, prep_cols, P if v_first is None else v_first, tables, prep_w, n_ctx, has_vres)
        if v_first is None:
            v_first = v

        o_rw_dirs = _rwkv_call(r, v, kk, lw, kd, a, n_ctx)
        log_gamma = -jnp.exp(ret_decay[l].astype(F32))
        lg = jnp.repeat(log_gamma, RET_HEAD, axis=-1).reshape(2, 1, W)
        o_ret_dirs = _ret_call(rq, rk, P, (o_ret + 2 * W) // W, lg, n_ctx)
        o_hg_dirs = _hgrn_call(hq, P, (o_hg + 3 * W) // W, hk, hlf, n_ctx)

        merge_w = (row(rwkv_lnx_w[l]), row(rwkv_lnx_b[l]), row(hgrn_norm_w[l]), ones64, ones128,
                   w_branch[l].astype(BF16), w_out[l].astype(BF16))
        xm = _merge_call(xa, P, gate_cols, o_rw_dirs, g, bonus, o_ret_dirs, o_hg_dirs, merge_w, modsel,
                         n_ctx, last)
        xa = _mlp_call(xm, modsel, norm2_w[l], mlp_w1[l].astype(BF16), mlp_w2[l].astype(BF16),
                       final_norm_w, n_ctx, (n_ctx // ROW_TILE) if last else 0, last)
        out = xa
    return out
```

```python
import functools

import jax
import jax.numpy as jnp
from jax import lax
from jax.experimental import pallas as pl
from jax.experimental.pallas import tpu as pltpu

F32 = jnp.float32
BF16 = jnp.bfloat16

NORM_EPS = 1e-6
RWKV_LN_EPS = 64e-5
L2_EPS = 1e-12
ROPE_BASE = 10000.0
GRID_W = 64

BRANCH_W = 512
RWKV_HEAD = 64
RET_HEAD = 128
HGRN_HEAD = 128
DECAY_LORA = 64
AAA_LORA = 64
GATE_LORA = 128
LORA_W = 2 * DECAY_LORA + 2 * AAA_LORA + GATE_LORA
VDOWN_PAD = 128

RWKV_CHUNK = 64
HGRN_CHUNK = 64
HGRN_SUB = 16
HGRN_STEP = 128
RET_CHUNK = 128
ROW_TILE = 256

VMEM_LIMIT = 56 * 1024 * 1024


def _sigmoid(x):
    return 1.0 / (1.0 + jnp.exp(-x))


def _dot(a, b):
    return jnp.dot(a.astype(BF16), b.astype(BF16), preferred_element_type=F32)


def _dot_nt(a, b):
    return lax.dot_general(a.astype(BF16), b.astype(BF16), (((1,), (1,)), ((), ())),
                           preferred_element_type=F32)


def _dot_tn(a, b):
    return lax.dot_general(a.astype(BF16), b.astype(BF16), (((0,), (0,)), ((), ())),
                           preferred_element_type=F32)


def _cumsum_rows(mask_f32, x):
    return jnp.dot(mask_f32, x, precision=lax.Precision.HIGHEST, preferred_element_type=F32)


def _head_sum(x, ones_bd):
    hi = x.astype(BF16)
    lo = (x - hi.astype(F32)).astype(BF16)
    return (jnp.dot(hi, ones_bd, preferred_element_type=F32)
            + jnp.dot(lo, ones_bd, preferred_element_type=F32))


def _ada_kernel(c_ref, w_ref, b_ref, o_ref):
    cv = c_ref[...]
    s = cv * _sigmoid(cv)
    o_ref[0] = _dot(s, w_ref[0]) + b_ref[0]


def _ada_call(cvec, ada_w, ada_b):
    L, D, D6 = ada_w.shape
    tn = D6 // 4
    return pl.pallas_call(
        _ada_kernel,
        grid=(L, D6 // tn),
        in_specs=[pl.BlockSpec((8, D), lambda l, j: (0, 0)),
                  pl.BlockSpec((1, D, tn), lambda l, j: (l, 0, j)),
                  pl.BlockSpec((1, 1, tn), lambda l, j: (l, 0, j))],
        out_specs=pl.BlockSpec((1, 8, tn), lambda l, j: (l, 0, j)),
        out_shape=jax.ShapeDtypeStruct((L, 8, D6), F32),
        compiler_params=pltpu.CompilerParams(vmem_limit_bytes=VMEM_LIMIT),
        name="ada_mod",
    )(cvec, ada_w, ada_b.reshape(L, 1, D6))


def _modulated_norm(xf, nw, mod_ref, is_ctx, which):
    y = xf * lax.rsqrt(jnp.mean(xf * xf, axis=-1, keepdims=True) + NORM_EPS) * nw
    sh = jnp.where(is_ctx, mod_ref[0, which:which + 1, :], mod_ref[0, 6 + which:7 + which, :])
    sc = jnp.where(is_ctx, mod_ref[0, which + 1:which + 2, :], mod_ref[0, 7 + which:8 + which, :])
    return y * (1.0 + sc) + sh


def _is_ctx_rows(tile_idx, tm, n_ctx):
    row = tile_idx * tm + lax.broadcasted_iota(jnp.int32, (tm, 1), 0)
    return row < n_ctx


def _proj_kernel(x_ref, mod_ref, nw_ref, w_ref, o_ref, h_ref, *, tm, n_ctx):
    i = pl.program_id(1)

    @pl.when(pl.program_id(2) == 0)
    def _():
        h = _modulated_norm(x_ref[0], nw_ref[...], mod_ref, _is_ctx_rows(i, tm, n_ctx), 0)
        h_ref[...] = h.astype(BF16)

    o_ref[0] = jnp.dot(h_ref[...], w_ref[...], preferred_element_type=F32)


def _proj_call(xa, modsel, nw, w, n_ctx, tm=768, n_tiles=4):
    B, N, D = xa.shape
    NP = w.shape[1]
    tn = NP // n_tiles
    return pl.pallas_call(
        functools.partial(_proj_kernel, tm=tm, n_ctx=n_ctx),
        grid=(B, N // tm, n_tiles),
        in_specs=[pl.BlockSpec((1, tm, D), lambda b, i, j: (b, i, 0)),
                  pl.BlockSpec((1, 12, D), lambda b, i, j: (b, 0, 0)),
                  pl.BlockSpec((1, D), lambda b, i, j: (0, 0)),
                  pl.BlockSpec((D, tn), lambda b, i, j: (0, j))],
        out_specs=pl.BlockSpec((1, tm, tn), lambda b, i, j: (b, i, j)),
        out_shape=jax.ShapeDtypeStruct((B, N, NP), F32),
        scratch_shapes=[pltpu.VMEM((tm, D), BF16)],
        compiler_params=pltpu.CompilerParams(
            dimension_semantics=("parallel", "parallel", "arbitrary"),
            vmem_limit_bytes=VMEM_LIMIT),
        name="in_proj",
    )(xa, modsel, nw.reshape(1, D), w)


def _shift_mix(cur, prev, nxt, mu, i, seq):
    tm = cur.shape[0]
    row = lax.broadcasted_iota(jnp.int32, (tm, 1), 0)
    is_ctx = i == 0
    period_mask = jnp.where(is_ctx, tm - 1, GRID_W - 1)
    col = row & period_mask
    ext = jnp.concatenate([prev, cur, nxt], axis=0)
    up = ext[0:tm]
    down = ext[2 * GRID_W:2 * GRID_W + tm]
    lat_row = (i - 1) * tm + row
    up = jnp.where(lat_row >= GRID_W, up, 0.0)
    down = jnp.where(lat_row < seq - GRID_W, down, 0.0)
    left = jnp.where(col != 0, pltpu.roll(cur, 1, 0), 0.0)
    right = jnp.where(col != period_mask, pltpu.roll(cur, tm - 1, 0), 0.0)
    w_lr = jnp.where(is_ctx, 0.5, 0.25)
    w_ud = jnp.where(is_ctx, 0.0, 0.25)
    shifted = w_lr * (left + right) + w_ud * (up + down)
    return cur + mu * (shifted - cur)


def _prep_kernel(pc_ref, pp_ref, pn_ref, lc_ref, lp_ref, ln_ref, hv_ref, vf_ref, rqk_ref, cs_ref, sn_ref,
                 hq_ref, hf0_ref, hf1_ref,
                 mum_ref, mul_ref, wup_ref, w0_ref, aup_ref, a0_ref, gup_ref, kk_w_ref, ka_ref, rk_ref,
                 v0_ref, vup_ref, ones_ref, lb_ref,
                 r_o, v_o, kk_o, lw_o, kd_o, a_o, g_o, bonus_o, rq_o, rk_o, hq_o, hk_o, hlf_o,
                 *, seq, has_vres):
    i = pl.program_id(1)
    W = BRANCH_W

    p = _shift_mix(pc_ref[0], pp_ref[0], pn_ref[0], mum_ref[...], i, seq)
    lo = _shift_mix(lc_ref[0], lp_ref[0], ln_ref[0], mul_ref[...], i, seq)
    r, k, v = p[:, :W], p[:, W:2 * W], p[:, 2 * W:]
    if has_vres:
        mix = _sigmoid(v0_ref[...] + _dot(hv_ref[0], vup_ref[...]))
        v = v + (vf_ref[0] - v) * mix
    r_o[0] = r
    v_o[0] = v
    gd = lo[:, 2 * DECAY_LORA + 2 * AAA_LORA:]
    g_o[0] = _dot(_sigmoid(gd), gup_ref[...])
    kx = k * kk_w_ref[...]
    kk_o[0] = kx * lax.rsqrt(_head_sum(kx * kx, ones_ref[...]) + L2_EPS)
    k_sum = None
    for d in range(2):
        wd = lo[:, d * DECAY_LORA:(d + 1) * DECAY_LORA]
        ad = lo[:, 2 * DECAY_LORA + d * AAA_LORA:2 * DECAY_LORA + (d + 1) * AAA_LORA]
        z = -(w0_ref[d:d + 1, :] + _dot(jnp.tanh(wd), wup_ref[d]))
        softplus = jnp.maximum(z, 0.0) + jnp.log(1.0 + jnp.exp(-jnp.abs(z)))
        lw_o[d, 0] = -jnp.exp(-softplus - 0.5)
        a = _sigmoid(a0_ref[d:d + 1, :] + _dot(ad, aup_ref[d]))
        a_o[d, 0] = a
        kd = k * (1.0 + (a - 1.0) * ka_ref[...])
        kd_o[d, 0] = kd
        k_sum = kd if k_sum is None else k_sum + kd
    bonus_o[0] = _head_sum(r * k_sum * rk_ref[...], ones_ref[...]) * v

    cosf = jnp.concatenate([cs_ref[...]] * (W // RET_HEAD), axis=1)
    sinf = jnp.concatenate([sn_ref[...]] * (W // RET_HEAD), axis=1)
    lane = lax.broadcasted_iota(jnp.int32, (1, W), 1)
    first_half = (lane & (RET_HEAD - 1)) < RET_HEAD // 2

    def rope(t):
        rot = jnp.where(first_half, pltpu.roll(t, W - RET_HEAD // 2, 1), pltpu.roll(t, RET_HEAD // 2, 1))
        return t * cosf + rot * sinf

    rq_o[0] = rope(rqk_ref[0, :, :W])
    rk_o[0] = rope(rqk_ref[0, :, W:]) * (RET_HEAD ** -0.5)

    q = hq_ref[0]
    hq_o[0] = q * _sigmoid(q)
    for d, f_ref in enumerate((hf0_ref, hf1_ref)):
        f = f_ref[0]
        e = jnp.exp(-jnp.abs(f))
        inv = 1.0 / (1.0 + e)
        sig_pos = jnp.where(f >= 0, inv, e * inv)
        sig_neg = jnp.where(f >= 0, e * inv, inv)
        lb = lb_ref[d:d + 1, :]
        hlf_o[d, 0] = jnp.log(lb + (1.0 - lb) * sig_pos)
        hk_o[d, 0] = (1.0 - lb) * sig_neg


def _prep_call(P, cols, v_first, tables, wts, n_ctx, has_vres):
    B, N, _ = P.shape
    W = BRANCH_W
    tm = ROW_TILE
    seq = N - n_ctx
    halo_per_tile = tm // GRID_W
    n_halo = N // GRID_W
    c_main, c_lora, c_vd, c_rqk, c_hq = cols

    def cur(width, col):
        return pl.BlockSpec((1, tm, width), lambda b, i: (b, i, col))

    def prev(width, col):
        return pl.BlockSpec((1, GRID_W, width), lambda b, i: (b, jnp.maximum(i * halo_per_tile - 1, 0), col))

    def nxt(width, col):
        return pl.BlockSpec((1, GRID_W, width),
                            lambda b, i: (b, jnp.minimum((i + 1) * halo_per_tile, n_halo - 1), col))

    def const(shape):
        return pl.BlockSpec(shape, lambda b, i: (0,) * len(shape))

    tok = pl.BlockSpec((1, tm, W), lambda b, i: (b, i, 0))
    tok2 = pl.BlockSpec((2, 1, tm, W), lambda b, i: (0, b, i, 0))
    cos, sin = tables
    in_specs = [cur(3 * W, c_main), prev(3 * W, c_main), nxt(3 * W, c_main),
                cur(LORA_W, c_lora), prev(LORA_W, c_lora), nxt(LORA_W, c_lora),
                cur(VDOWN_PAD, c_vd), tok, cur(2 * W, c_rqk),
                pl.BlockSpec((tm, RET_HEAD), lambda b, i: (i, 0)),
                pl.BlockSpec((tm, RET_HEAD), lambda b, i: (i, 0)),
                cur(W, c_hq), cur(W, c_hq + 1), cur(W, c_hq + 2)]
    in_specs += [const(w.shape) for w in wts]
    one = jax.ShapeDtypeStruct((B, N, W), F32)
    two = jax.ShapeDtypeStruct((2, B, N, W), F32)
    out_shape = [one, one, one, two, two, two, one, one, one, one, one, two, two]
    out_specs = [tok, tok, tok, tok2, tok2, tok2, tok, tok, tok, tok, tok, tok2, tok2]
    return pl.pallas_call(
        functools.partial(_prep_kernel, seq=seq, has_vres=has_vres),
        grid=(B, N // tm),
        in_specs=in_specs,
        out_specs=out_specs,
        out_shape=out_shape,
        compiler_params=pltpu.CompilerParams(
            dimension_semantics=("parallel", "parallel"), vmem_limit_bytes=VMEM_LIMIT),
        name="branch_prep",
    )(P, P, P, P, P, P, P, v_first, P, cos, sin, P, P, P, *wts)


def _merge_kernel(x_ref, g0_ref, g1_ref, g2_ref, orwf_ref, orwb_ref, g_ref, bonus_ref, oretf_ref, oretb_ref,
                  rg_ref, ohgf_ref, ohgb_ref, hg_ref,
                  lnw_ref, lnb_ref, hnw_ref, m64_ref, m128_ref, wb_ref, wo_ref, mod_ref, o_ref,
                  *, tm, n_ctx, tile_off):
    i = pl.program_id(1) + tile_off

    o = orwf_ref[0] + orwb_ref[0]
    cen = o - _head_sum(o, m64_ref[...]) * (1.0 / RWKV_HEAD)
    var = _head_sum(cen * cen, m64_ref[...]) * (1.0 / RWKV_HEAD)
    y_rw = (cen * lax.rsqrt(var + RWKV_LN_EPS) * lnw_ref[...] + lnb_ref[...] + bonus_ref[0]) * g_ref[0]
    o = oretf_ref[0] + oretb_ref[0]
    ms = _head_sum(o * o, m128_ref[...]) * (1.0 / RET_HEAD)
    rg = rg_ref[0]
    y_ret = o * lax.rsqrt(ms + NORM_EPS) * (rg * _sigmoid(rg))
    o = ohgf_ref[0] + ohgb_ref[0]
    ms = _head_sum(o * o, m128_ref[...]) * (1.0 / HGRN_HEAD)
    hg = hg_ref[0]
    y_hg = o * lax.rsqrt(ms + NORM_EPS) * hnw_ref[...] * (hg * _sigmoid(hg))

    merged = None
    for b, (y, gate_ref) in enumerate(((y_rw, g0_ref), (y_ret, g1_ref), (y_hg, g2_ref))):
        z = _sigmoid(gate_ref[0]) * _dot(y, wb_ref[b])
        merged = z if merged is None else merged + z
    out = _dot(merged, wo_ref[...])
    g1 = jnp.where(_is_ctx_rows(i, tm, n_ctx), mod_ref[0, 2:3, :], mod_ref[0, 8:9, :])
    o_ref[0] = x_ref[0] + g1 * out


def _merge_call(xa, P, gate_cols, o_rw, g, bonus, o_ret, o_hg, wts, modsel, n_ctx, latent_only):
    B, N, D = xa.shape
    W = BRANCH_W
    tm = ROW_TILE
    off = n_ctx // tm if latent_only else 0
    n_rows = N - off * tm
    c_gate, c_rg, c_hg = gate_cols

    def col(width, c):
        return pl.BlockSpec((1, tm, width), lambda b, i: (b, i + off, c))

    def const(shape):
        return pl.BlockSpec(shape, lambda b, i: (0,) * len(shape))

    tok = col(W, 0)
    return pl.pallas_call(
        functools.partial(_merge_kernel, tm=tm, n_ctx=n_ctx, tile_off=off),
        grid=(B, n_rows // tm),
        in_specs=[col(D, 0), col(D, c_gate), col(D, c_gate + 1), col(D, c_gate + 2),
                  tok, tok, tok, tok, tok, tok, col(W, c_rg), tok, tok, col(W, c_hg)]
                 + [const(w.shape) for w in wts]
                 + [pl.BlockSpec((1, 12, D), lambda b, i: (b, 0, 0))],
        out_specs=pl.BlockSpec((1, tm, D), lambda b, i: (b, i, 0)),
        out_shape=jax.ShapeDtypeStruct((B, n_rows, D), F32),
        compiler_params=pltpu.CompilerParams(
            dimension_semantics=("parallel", "parallel"), vmem_limit_bytes=VMEM_LIMIT),
        name="merge_out",
    )(xa, P, P, P, o_rw[0], o_rw[1], g, bonus, o_ret[0], o_ret[1], P, o_hg[0], o_hg[1], P, *wts, modsel)


def _mlp_kernel(x_ref, mod_ref, nw_ref, w1_ref, w2_ref, fw_ref, o_ref, *, tm, n_ctx, tile_off, final):
    i = pl.program_id(1) + tile_off
    is_ctx = _is_ctx_rows(i, tm, n_ctx)
    xf = x_ref[0]
    h = _modulated_norm(xf, nw_ref[...], mod_ref, is_ctx, 3)
    a = jnp.maximum(_dot(h, w1_ref[...]), 0.0)
    out = _dot(a * a, w2_ref[...])
    g2 = jnp.where(is_ctx, mod_ref[0, 5:6, :], mod_ref[0, 11:12, :])
    xn = xf + g2 * out
    if final:
        xn = xn * lax.rsqrt(jnp.mean(xn * xn, axis=-1, keepdims=True) + NORM_EPS) * fw_ref[...]
    o_ref[0] = xn


def _mlp_call(xa, modsel, nw, w1, w2, fw, n_ctx, tile_off, final):
    B, n_rows, D = xa.shape
    DF = w1.shape[1]
    tm = ROW_TILE
    return pl.pallas_call(
        functools.partial(_mlp_kernel, tm=tm, n_ctx=n_ctx, tile_off=tile_off, final=final),
        grid=(B, n_rows // tm),
        in_specs=[pl.BlockSpec((1, tm, D), lambda b, i: (b, i, 0)),
                  pl.BlockSpec((1, 12, D), lambda b, i: (b, 0, 0)),
                  pl.BlockSpec((1, D), lambda b, i: (0, 0)),
                  pl.BlockSpec((D, DF), lambda b, i: (0, 0)),
                  pl.BlockSpec((DF, D), lambda b, i: (0, 0)),
                  pl.BlockSpec((1, D), lambda b, i: (0, 0))],
        out_specs=pl.BlockSpec((1, tm, D), lambda b, i: (b, i, 0)),
        out_shape=jax.ShapeDtypeStruct((B, n_rows, D), F32),
        compiler_params=pltpu.CompilerParams(
            dimension_semantics=("parallel", "parallel"), vmem_limit_bytes=VMEM_LIMIT),
        name="mlp",
    )(xa, modsel, nw.reshape(1, D), w1, w2, fw.reshape(1, D))


def _order_masks(d, n):
    t = lax.broadcasted_iota(jnp.int32, (n, n), 0)
    s = lax.broadcasted_iota(jnp.int32, (n, n), 1)
    lead = (t - s) * (1 - 2 * d)
    return lead > 0, lead >= 0


def _scan_specs(T, W, n_ctx, N):
    n_ctx_blocks, n_blocks = n_ctx // T, N // T

    def block(d, j):
        if d == 0:
            return j
        return jnp.where(j < n_ctx_blocks, n_ctx_blocks - 1 - j, n_blocks - 1 + n_ctx_blocks - j)

    def shared(d, col=0):
        return pl.BlockSpec((1, T, W), lambda b, j: (b, block(d, j), col))

    def per_dir(d):
        return pl.BlockSpec((1, 1, T, W), lambda b, j: (d, b, block(d, j), 0))

    return shared, per_dir


def _scan_call(kernel_fn, name, T, specs, args, state_shape, n_ctx):
    B, N, W = args[0].shape[0], args[0].shape[1], BRANCH_W
    shared, _ = _scan_specs(T, W, n_ctx, N)
    out = jax.ShapeDtypeStruct((B, N, W), F32)
    return pl.pallas_call(
        kernel_fn,
        grid=(B, N // T),
        in_specs=specs,
        out_specs=[shared(0), shared(1)],
        out_shape=[out, out],
        scratch_shapes=[pltpu.VMEM(state_shape, F32)],
        compiler_params=pltpu.CompilerParams(
            dimension_semantics=("parallel", "arbitrary"), vmem_limit_bytes=VMEM_LIMIT),
        name=name,
    )(*args)


def _rwkv_kernel(*refs):
    fwd_in, bwd_in, (of_ref, ob_ref, s_ref) = refs[0:6], refs[6:12], refs[12:]

    @pl.when(pl.program_id(1) == 0)
    def _():
        s_ref[...] = jnp.zeros_like(s_ref)

    ins = [tuple(ref[0] if len(ref.shape) == 3 else ref[0, 0] for ref in side) for side in (fwd_in, bwd_in)]
    n_groups = s_ref.shape[1]
    S = {(d, g): s_ref[d, g] for d in range(2) for g in range(n_groups)}
    outs, S = _rwkv_step(ins, S, n_groups)
    of_ref[0] = outs[0]
    ob_ref[0] = outs[1]
    for key, s_new in S.items():
        s_ref[key] = s_new


def _rwkv_step(ins, S, n_groups):
    C = RWKV_CHUNK
    HG = 4 * RWKV_HEAD
    chains = [(d, g) for d in range(2) for g in range(n_groups)]

    rb = lax.broadcasted_iota(jnp.int32, (4 * C, HG), 0) // C
    lb = lax.broadcasted_iota(jnp.int32, (4 * C, HG), 1) // RWKV_HEAD
    same_head = rb == lb
    head_mask = same_head.astype(F32).astype(BF16)
    t_idx = lax.broadcasted_iota(jnp.int32, (C, HG), 0)
    s_idx = lax.broadcasted_iota(jnp.int32, (C, HG), 1) % C

    def bd(x):
        xb = x.astype(BF16)
        return jnp.concatenate([xb, xb, xb, xb], axis=0) * head_mask

    G = [_cumsum_rows(_order_masks(d, C)[1].astype(F32), ins[d][3]) for d in range(2)]
    strict, incl, eye = {}, {}, {}
    q_t, r_t, p_t, k_t, p_hat, k_hat, v_g, e_end = {}, {}, {}, {}, {}, {}, {}, {}
    for d in range(2):
        r, v, kk, lw, k, a = ins[d]
        lead = (t_idx - s_idx) * (1 - 2 * d)
        strict[d], incl[d], eye[d] = lead > 0, lead >= 0, (lead == 0).astype(F32)
        g_end = jnp.sum(lw, axis=0, keepdims=True)
        e_neg = jnp.exp(-G[d])
        end = jnp.exp(g_end)
        full = dict(q=kk * jnp.exp(G[d] - lw), r=r * jnp.exp(G[d]), p=-(a * kk) * e_neg, k=k * e_neg, v=v)
        for g in range(n_groups):
            sl = slice(g * HG, (g + 1) * HG)
            c = (d, g)
            q_t[c], r_t[c], p_t[c], k_t[c], v_g[c] = (full[n][:, sl] for n in "qrpkv")
            e_end[c] = end[:, sl]
            p_hat[c], k_hat[c] = p_t[c] * e_end[c], k_t[c] * e_end[c]

    bv = {c: bd(v_g[c]) for c in chains}
    qr = {c: jnp.concatenate([q_t[c], r_t[c]], axis=0).astype(BF16) for c in chains}
    a_p = {c: _dot_nt(qr[c], bd(p_t[c])) for c in chains}
    a_k = {c: _dot_nt(qr[c], bd(k_t[c])) for c in chains}
    n_qp = {c: jnp.where(strict[c[0]], a_p[c][:C], 0.0) for c in chains}
    a_qk = {c: jnp.where(strict[c[0]], a_k[c][:C], 0.0) for c in chains}
    a_rp = {c: jnp.where(incl[c[0]], a_p[c][C:], 0.0) for c in chains}
    a_rk = {c: jnp.where(incl[c[0]], a_k[c][C:], 0.0) for c in chains}

    R = {c: _dot(n_qp[c], bd(n_qp[c])) for c in chains}
    T = {c: eye[c[0]] + n_qp[c] for c in chains}
    av = {c: _dot(a_qk[c], bv[c]) for c in chains}
    span = 4
    while span < C:
        RT = {c: _dot(jnp.concatenate([R[c], T[c]], axis=0), bd(R[c])) for c in chains}
        R = {c: RT[c][:C] for c in chains}
        T = {c: T[c] + RT[c][C:] for c in chains}
        span *= 2
    TR = {c: _dot(T[c], bd(R[c])) for c in chains}
    Tb = {c: (T[c] + TR[c]).astype(BF16) for c in chains}

    Sb = {c: S[c].astype(BF16) for c in chains}
    w = {c: _dot(Tb[c], bd(q_t[c])) for c in chains}
    u0 = {c: _dot(Tb[c], bd(av[c])) for c in chains}
    u = {c: _dot_nt(w[c], Sb[c]) + u0[c] for c in chains}
    o = {c: _dot(a_rp[c], bd(u[c])) + _dot(a_rk[c], bv[c]) + _dot_nt(r_t[c], Sb[c]) for c in chains}
    upd = {c: _dot_tn(jnp.concatenate([u[c], v_g[c]], axis=0),
                      jnp.concatenate([p_hat[c], k_hat[c]], axis=0)) for c in chains}
    S_new = {c: S[c] * e_end[c] + jnp.where(same_head, upd[c], 0.0) for c in chains}
    outs = [jnp.concatenate([o[(d, g)] for g in range(n_groups)], axis=1) for d in range(2)]
    return outs, S_new


def _rwkv_call(r, v, kk, lw, k, a, n_ctx):
    B, N, W = r.shape
    C = RWKV_CHUNK
    assert C == RWKV_HEAD and W % (4 * RWKV_HEAD) == 0
    shared, per_dir = _scan_specs(C, W, n_ctx, N)
    specs = [s for d in range(2) for s in (shared(d), shared(d), shared(d), per_dir(d), per_dir(d), per_dir(d))]
    args = (r, v, kk, lw, k, a) * 2
    n_groups = W // (4 * RWKV_HEAD)
    return _scan_call(_rwkv_kernel, "rwkv7_scan", C, specs, args,
                      (2, n_groups, 4 * RWKV_HEAD, 4 * RWKV_HEAD), n_ctx)


def _ret_kernel(qf_ref, kf_ref, vf_ref, qb_ref, kb_ref, vb_ref, lg_ref, of_ref, ob_ref, s_ref):
    C = RET_CHUNK
    Dh = RET_HEAD

    @pl.when(pl.program_id(1) == 0)
    def _():
        s_ref[...] = jnp.zeros_like(s_ref)

    t = lax.broadcasted_iota(jnp.int32, (C, C), 0)
    s = lax.broadcasted_iota(jnp.int32, (C, C), 1)
    dist = jnp.abs(t - s).astype(F32)
    tcol = lax.broadcasted_iota(jnp.int32, (C, 1), 0)
    io = ((qf_ref, kf_ref, vf_ref, of_ref), (qb_ref, kb_ref, vb_ref, ob_ref))
    n_heads = BRANCH_W // Dh
    chains = [(d, h) for d in range(2) for h in range(n_heads)]
    incl = [_order_masks(d, C)[1] for d in range(2)]
    n_t = [(tcol + 1 if d == 0 else C - tcol).astype(F32) for d in range(2)]

    def head(ref, h):
        return ref[0, :, h * Dh:(h + 1) * Dh]

    scores = {(d, h): _dot_nt(head(io[d][0], h), head(io[d][1], h)) for d, h in chains}
    outs, H = {}, {}
    for d, h in chains:
        lg = lg_ref[d, :, h * Dh:h * Dh + C]
        lgd = lg_ref[d, :, h * Dh:(h + 1) * Dh]
        A = scores[d, h] * jnp.where(incl[d], jnp.exp(lg * dist), 0.0)
        H[d, h] = s_ref[d, h]
        outs[d, h] = _dot(A, head(io[d][2], h)) + _dot(head(io[d][0], h) * jnp.exp(lgd * n_t[d]), H[d, h])
    for d, h in chains:
        lgd = lg_ref[d, :, h * Dh:(h + 1) * Dh]
        k_hat = head(io[d][1], h) * jnp.exp(lgd * (C - n_t[d]))
        s_ref[d, h] = H[d, h] * jnp.exp(lgd * C) + _dot_tn(k_hat, head(io[d][2], h))
    for d in range(2):
        io[d][3][0] = jnp.concatenate([outs[d, h] for h in range(n_heads)], axis=1)


def _ret_call(q, k, v, v_col, lg, n_ctx):
    B, N, W = q.shape
    C = RET_CHUNK
    assert C <= RET_HEAD
    shared, _ = _scan_specs(C, W, n_ctx, N)
    specs = [s for d in range(2) for s in (shared(d), shared(d), shared(d, v_col))]
    specs.append(pl.BlockSpec((2, 1, W), lambda b, j: (0, 0, 0)))
    return _scan_call(_ret_kernel, "retention_scan", C, specs, (q, k, v, q, k, v, lg),
                      (2, W // RET_HEAD, RET_HEAD, RET_HEAD), n_ctx)


def _hgrn_kernel(qf_ref, vf_ref, kf_ref, lff_ref, qb_ref, vb_ref, kb_ref, lfb_ref, of_ref, ob_ref, s_ref):
    C = HGRN_CHUNK
    n_heads = s_ref.shape[1]

    @pl.when(pl.program_id(1) == 0)
    def _():
        s_ref[...] = jnp.zeros_like(s_ref)

    io = ((qf_ref, vf_ref, kf_ref, lff_ref, of_ref), (qb_ref, vb_ref, kb_ref, lfb_ref, ob_ref))
    n_chunks = HGRN_STEP // C
    order = [list(range(n_chunks)), list(reversed(range(n_chunks)))]
    blocks = [(d, c) for d in range(2) for c in range(n_chunks)]

    pre = {}
    G = {(d, c): _cumsum_rows(_order_masks(d, C)[1].astype(F32), io[d][3][0, 0, c * C:(c + 1) * C, :])
         for d, c in blocks}
    for d, c in blocks:
        rows = slice(c * C, (c + 1) * C)
        q_ref, v_ref, k_ref, lf_ref, _ = io[d]
        pre[d, c] = _hgrn_intra(d, q_ref[0, rows, :], k_ref[0, 0, rows, :], v_ref[0, rows, :],
                                lf_ref[0, 0, rows, :], G[d, c])
    heads = range(n_heads)
    intra = {(d, c, h): _dot(pre[d, c]["A"][h], pre[d, c]["v"][h]) for d, c in blocks for h in heads}

    S = {(d, h): s_ref[d, h] for d in range(2) for h in heads}
    for i in range(n_chunks):
        inter, S_next = {}, {}
        for d in range(2):
            blk = pre[d, order[d][i]]
            for h in heads:
                inter[d, h] = _dot_nt(blk["q_full"][h], S[d, h])
                S_next[d, h] = S[d, h] * blk["e_end"][h] + _dot_tn(blk["v"][h], blk["k_hat"][h])
        for d in range(2):
            c = order[d][i]
            io[d][4][0, c * C:(c + 1) * C, :] = jnp.concatenate(
                [intra[d, c, h] + inter[d, h] for h in heads], axis=1)
        S = S_next
    for key, s_new in S.items():
        s_ref[key] = s_new


def _hgrn_intra(d, q, k, v, lf, G):
    C, SUB = HGRN_CHUNK, HGRN_SUB
    Dh = HGRN_HEAD
    n_sub = C // SUB
    _, incl = _order_masks(d, C)
    g_end = jnp.sum(lf, axis=0, keepdims=True)
    W = q.shape[-1]
    anchors = [G[I * SUB + SUB // 2:I * SUB + SUB // 2 + 1, :] for I in range(n_sub)]
    g_anchor = jnp.concatenate([jnp.broadcast_to(a, (SUB, W)) for a in anchors], axis=0)
    q_a = q * jnp.exp(G - g_anchor)
    q_full = q * jnp.exp(G)
    k_hat = k * jnp.exp(g_end - G)
    sub_of_row = lax.broadcasted_iota(jnp.int32, (C, 1), 0) // SUB
    k_anch = []
    for I in range(n_sub):
        visible = (I - sub_of_row) * (1 - 2 * d) >= 0
        k_anch.append(k * jnp.exp(jnp.where(visible, anchors[I] - G, 0.0)))
    e_end = jnp.exp(g_end)
    out = dict(A=[], v=[], q_full=[], k_hat=[], e_end=[])
    for h in range(W // Dh):
        sl = slice(h * Dh, (h + 1) * Dh)
        A = jnp.concatenate([_dot_nt(q_a[I * SUB:(I + 1) * SUB, sl], k_anch[I][:, sl]) for I in range(n_sub)],
                            axis=0)
        out["A"].append(jnp.where(incl, A, 0.0))
        out["v"].append(v[:, sl])
        out["q_full"].append(q_full[:, sl])
        out["k_hat"].append(k_hat[:, sl])
        out["e_end"].append(e_end[:, sl])
    return out


def _hgrn_call(q, v, v_col, k, lf, n_ctx):
    B, N, W = q.shape
    T = HGRN_STEP
    shared, per_dir = _scan_specs(T, W, n_ctx, N)
    specs = [s for d in range(2) for s in (shared(d), shared(d, v_col), per_dir(d), per_dir(d))]
    return _scan_call(_hgrn_kernel, "hgrn2_scan", T, specs, (q, v, k, lf) * 2,
                      (2, W // HGRN_HEAD, HGRN_HEAD, HGRN_HEAD), n_ctx)


def _rope_tables(n, head):
    half = head // 2
    inv_freq = ROPE_BASE ** (-jnp.arange(half, dtype=F32) / half)
    ang = jnp.arange(n).astype(F32)[:, None] * inv_freq[None, :]
    cos, sin = jnp.cos(ang), jnp.sin(ang)
    return jnp.concatenate([cos, cos], axis=1), jnp.concatenate([-sin, sin], axis=1)


def _block_diag_ones(width, head):
    idx = jnp.arange(width) // head
    return (idx[:, None] == idx[None, :]).astype(BF16)


def _permute_w_in(w_in_l, v_down_l):
    D = w_in_l.shape[0]
    n_rw, n_ret, n_hg = 3 * BRANCH_W + LORA_W, 4 * BRANCH_W, 5 * BRANCH_W
    rw = w_in_l[:, :n_rw]
    ret = w_in_l[:, n_rw:n_rw + n_ret]
    hg = w_in_l[:, n_rw + n_ret:n_rw + n_ret + n_hg]
    gate = w_in_l[:, n_rw + n_ret + n_hg:]
    extra = jnp.zeros((D, VDOWN_PAD), F32)
    if v_down_l is not None:
        extra = extra.at[:, :v_down_l.shape[1]].set(v_down_l)
    return jnp.concatenate([gate, rw, extra, ret, hg], axis=1).astype(BF16)


def kernel(x, c, ctx, c_ctx, ada_w, ada_b, norm1_w, norm2_w, w_in, rwkv_mu, rwkv_w0, rwkv_w_up, rwkv_a0, rwkv_a_up, rwkv_g_up, rwkv_k_k, rwkv_k_a, rwkv_r_k, rwkv_lnx_w, rwkv_lnx_b, rwkv_v0, rwkv_v_down, rwkv_v_up, ret_decay, hgrn_lb, hgrn_norm_w, w_branch, w_out, mlp_w1, mlp_w2, final_norm_w):
    B, SEQ, D = x.shape
    n_ctx = ctx.shape[1]
    N = n_ctx + SEQ
    L = ada_w.shape[0]
    W = BRANCH_W
    assert n_ctx == ROW_TILE and SEQ % ROW_TILE == 0 and D == 2 * W
    o_rw = 3 * D
    o_lora = o_rw + 3 * W
    o_vd = o_lora + LORA_W
    o_ret = o_vd + VDOWN_PAD
    o_hg = o_ret + 4 * W
    prep_cols = (o_rw // (3 * W), o_lora // LORA_W, o_vd // VDOWN_PAD, o_ret // (2 * W), o_hg // W)
    gate_cols = (0, (o_ret + 3 * W) // W, (o_hg + 4 * W) // W)
    assert o_rw % (3 * W) == 0 and o_lora % LORA_W == 0 and o_ret % (2 * W) == 0

    xa = jnp.concatenate([ctx, x], axis=1)
    cvec = jnp.concatenate([c, c_ctx[None], jnp.zeros((8 - B - 1, D), F32)], axis=0)
    mod = _ada_call(cvec, ada_w, ada_b)
    sm = jax.nn.softmax(hgrn_lb.astype(F32), axis=0)
    hgrn_lower = jnp.cumsum(sm, axis=0) - sm[0:1]
    tables = _rope_tables(N, RET_HEAD)
    ones64 = _block_diag_ones(W, RWKV_HEAD)
    ones128 = _block_diag_ones(W, RET_HEAD)
    row = lambda t: t.reshape(1, -1)
    v_first = None
    out = None

    for l in range(L):
        last = l == L - 1
        mod_c = jnp.broadcast_to(mod[l, B].reshape(1, 6, D), (B, 6, D))
        modsel = jnp.concatenate([mod_c, mod[l, :B].reshape(B, 6, D)], axis=1)
        w_l = _permute_w_in(w_in[l], None if l == 0 else rwkv_v_down[l - 1])
        P = _proj_call(xa, modsel, norm1_w[l], w_l, n_ctx)

        has_vres = l > 0
        if has_vres:
            v0 = row(rwkv_v0[l - 1])
            v_up = jnp.zeros((VDOWN_PAD, W), F32).at[:rwkv_v_up.shape[1]].set(rwkv_v_up[l - 1]).astype(BF16)
        else:
            v0 = jnp.zeros((1, W), F32)
            v_up = jnp.zeros((VDOWN_PAD, W), BF16)
        prep_w = (row(rwkv_mu[l, :3 * W]), row(rwkv_mu[l, 3 * W:]),
                  rwkv_w_up[l].astype(BF16), rwkv_w0[l], rwkv_a_up[l].astype(BF16), rwkv_a0[l],
                  rwkv_g_up[l].astype(BF16), row(rwkv_k_k[l]), row(rwkv_k_a[l]), row(rwkv_r_k[l]),
                  v0, v_up, ones64, hgrn_lower[l])
        (r, v, kk, lw, kd, a, g, bonus, rq, rk, hq, hk, hlf) = _prep_call(
            P, prep_cols, P if v_first is None else v_first, tables, prep_w, n_ctx, has_vres)
        if v_first is None:
            v_first = v

        o_rw_dirs = _rwkv_call(r, v, kk, lw, kd, a, n_ctx)
        log_gamma = -jnp.exp(ret_decay[l].astype(F32))
        lg = jnp.repeat(log_gamma, RET_HEAD, axis=-1).reshape(2, 1, W)
        o_ret_dirs = _ret_call(rq, rk, P, (o_ret + 2 * W) // W, lg, n_ctx)
        o_hg_dirs = _hgrn_call(hq, P, (o_hg + 3 * W) // W, hk, hlf, n_ctx)

        merge_w = (row(rwkv_lnx_w[l]), row(rwkv_lnx_b[l]), row(hgrn_norm_w[l]), ones64, ones128,
                   w_branch[l].astype(BF16), w_out[l].astype(BF16))
        xm = _merge_call(xa, P, gate_cols, o_rw_dirs, g, bonus, o_ret_dirs, o_hg_dirs, merge_w, modsel,
                         n_ctx, last)
        xa = _mlp_call(xm, modsel, norm2_w[l], mlp_w1[l].astype(BF16), mlp_w2[l].astype(BF16),
                       final_norm_w, n_ctx, (n_ctx // ROW_TILE) if last else 0, last)
        out = xa
    return out
```

```python
import functools

import jax
import jax.numpy as jnp
from jax import lax
from jax.experimental import pallas as pl
from jax.experimental.pallas import tpu as pltpu

F32 = jnp.float32
BF16 = jnp.bfloat16

NORM_EPS = 1e-6
RWKV_LN_EPS = 64e-5
L2_EPS = 1e-12
ROPE_BASE = 10000.0
GRID_W = 64

BRANCH_W = 512
RWKV_HEAD = 64
RET_HEAD = 128
HGRN_HEAD = 128
DECAY_LORA = 64
AAA_LORA = 64
GATE_LORA = 128
LORA_W = 2 * DECAY_LORA + 2 * AAA_LORA + GATE_LORA
VDOWN_PAD = 128

RWKV_CHUNK = 64
HGRN_CHUNK = 64
HGRN_SUB = 16
HGRN_STEP = 128
RET_CHUNK = 128
ROW_TILE = 256

VMEM_LIMIT = 56 * 1024 * 1024


def _sigmoid(x):
    return 1.0 / (1.0 + jnp.exp(-x))


def _dot(a, b):
    return jnp.dot(a.astype(BF16), b.astype(BF16), preferred_element_type=F32)


def _dot_nt(a, b):
    return lax.dot_general(a.astype(BF16), b.astype(BF16), (((1,), (1,)), ((), ())),
                           preferred_element_type=F32)


def _dot_tn(a, b):
    return lax.dot_general(a.astype(BF16), b.astype(BF16), (((0,), (0,)), ((), ())),
                           preferred_element_type=F32)


def _cumsum_rows(mask_f32, x):
    return jnp.dot(mask_f32, x, precision=lax.Precision.HIGHEST, preferred_element_type=F32)


def _head_sum(x, ones_bd):
    hi = x.astype(BF16)
    lo = (x - hi.astype(F32)).astype(BF16)
    return (jnp.dot(hi, ones_bd, preferred_element_type=F32)
            + jnp.dot(lo, ones_bd, preferred_element_type=F32))


def _ada_kernel(c_ref, w_ref, b_ref, o_ref):
    cv = c_ref[...]
    s = cv * _sigmoid(cv)
    o_ref[0] = _dot(s, w_ref[0]) + b_ref[0]


def _ada_call(cvec, ada_w, ada_b):
    L, D, D6 = ada_w.shape
    tn = D6 // 4
    return pl.pallas_call(
        _ada_kernel,
        grid=(L, D6 // tn),
        in_specs=[pl.BlockSpec((8, D), lambda l, j: (0, 0)),
                  pl.BlockSpec((1, D, tn), lambda l, j: (l, 0, j)),
                  pl.BlockSpec((1, 1, tn), lambda l, j: (l, 0, j))],
        out_specs=pl.BlockSpec((1, 8, tn), lambda l, j: (l, 0, j)),
        out_shape=jax.ShapeDtypeStruct((L, 8, D6), F32),
        compiler_params=pltpu.CompilerParams(vmem_limit_bytes=VMEM_LIMIT),
        name="ada_mod",
    )(cvec, ada_w, ada_b.reshape(L, 1, D6))


def _modulated_norm(xf, nw, mod_ref, is_ctx, which):
    y = xf * lax.rsqrt(jnp.mean(xf * xf, axis=-1, keepdims=True) + NORM_EPS) * nw
    sh = jnp.where(is_ctx, mod_ref[0, which:which + 1, :], mod_ref[0, 6 + which:7 + which, :])
    sc = jnp.where(is_ctx, mod_ref[0, which + 1:which + 2, :], mod_ref[0, 7 + which:8 + which, :])
    return y * (1.0 + sc) + sh


def _is_ctx_rows(tile_idx, tm, n_ctx):
    row = tile_idx * tm + lax.broadcasted_iota(jnp.int32, (tm, 1), 0)
    return row < n_ctx


def _proj_kernel(x_ref, mod_ref, nw_ref, w_ref, o_ref, h_ref, *, tm, n_ctx):
    i = pl.program_id(1)

    @pl.when(pl.program_id(2) == 0)
    def _():
        h = _modulated_norm(x_ref[0], nw_ref[...], mod_ref, _is_ctx_rows(i, tm, n_ctx), 0)
        h_ref[...] = h.astype(BF16)

    o_ref[0] = jnp.dot(h_ref[...], w_ref[...], preferred_element_type=F32)


def _proj_call(xa, modsel, nw, w, n_ctx, tm=768, n_tiles=4):
    B, N, D = xa.shape
    NP = w.shape[1]
    tn = NP // n_tiles
    return pl.pallas_call(
        functools.partial(_proj_kernel, tm=tm, n_ctx=n_ctx),
        grid=(B, N // tm, n_tiles),
        in_specs=[pl.BlockSpec((1, tm, D), lambda b, i, j: (b, i, 0)),
                  pl.BlockSpec((1, 12, D), lambda b, i, j: (b, 0, 0)),
                  pl.BlockSpec((1, D), lambda b, i, j: (0, 0)),
                  pl.BlockSpec((D, tn), lambda b, i, j: (0, j))],
        out_specs=pl.BlockSpec((1, tm, tn), lambda b, i, j: (b, i, j)),
        out_shape=jax.ShapeDtypeStruct((B, N, NP), F32),
        scratch_shapes=[pltpu.VMEM((tm, D), BF16)],
        compiler_params=pltpu.CompilerParams(
            dimension_semantics=("parallel", "parallel", "arbitrary"),
            vmem_limit_bytes=VMEM_LIMIT),
        name="in_proj",
    )(xa, modsel, nw.reshape(1, D), w)


def _shift_mix(cur, prev, nxt, mu, i, seq):
    tm = cur.shape[0]
    row = lax.broadcasted_iota(jnp.int32, (tm, 1), 0)
    is_ctx = i == 0
    period_mask = jnp.where(is_ctx, tm - 1, GRID_W - 1)
    col = row & period_mask
    ext = jnp.concatenate([prev, cur, nxt], axis=0)
    up = ext[0:tm]
    down = ext[2 * GRID_W:2 * GRID_W + tm]
    lat_row = (i - 1) * tm + row
    up = jnp.where(lat_row >= GRID_W, up, 0.0)
    down = jnp.where(lat_row < seq - GRID_W, down, 0.0)
    left = jnp.where(col != 0, pltpu.roll(cur, 1, 0), 0.0)
    right = jnp.where(col != period_mask, pltpu.roll(cur, tm - 1, 0), 0.0)
    w_lr = jnp.where(is_ctx, 0.5, 0.25)
    w_ud = jnp.where(is_ctx, 0.0, 0.25)
    shifted = w_lr * (left + right) + w_ud * (up + down)
    return cur + mu * (shifted - cur)


def _prep_kernel(pc_ref, pp_ref, pn_ref, lc_ref, lp_ref, ln_ref, hv_ref, vf_ref, rqk_ref, cs_ref, sn_ref,
                 hq_ref, hf0_ref, hf1_ref,
                 mum_ref, mul_ref, wup_ref, w0_ref, aup_ref, a0_ref, gup_ref, kk_w_ref, ka_ref, rk_ref,
                 v0_ref, vup_ref, ones_ref, lb_ref,
                 r_o, v_o, kk_o, lw_o, kd_o, a_o, g_o, bonus_o, rq_o, rk_o, hq_o, hk_o, hlf_o,
                 *, seq, has_vres):
    i = pl.program_id(1)
    W = BRANCH_W

    p = _shift_mix(pc_ref[0], pp_ref[0], pn_ref[0], mum_ref[...], i, seq)
    lo = _shift_mix(lc_ref[0], lp_ref[0], ln_ref[0], mul_ref[...], i, seq)
    r, k, v = p[:, :W], p[:, W:2 * W], p[:, 2 * W:]
    if has_vres:
        mix = _sigmoid(v0_ref[...] + _dot(hv_ref[0], vup_ref[...]))
        v = v + (vf_ref[0].astype(F32) - v) * mix
    r_o[0] = r.astype(r_o.dtype)
    v_o[0] = v.astype(v_o.dtype)
    gd = lo[:, 2 * DECAY_LORA + 2 * AAA_LORA:]
    g_o[0] = _dot(_sigmoid(gd), gup_ref[...]).astype(g_o.dtype)
    kx = k * kk_w_ref[...]
    kk_o[0] = (kx * lax.rsqrt(_head_sum(kx * kx, ones_ref[...]) + L2_EPS)).astype(kk_o.dtype)
    k_sum = None
    for d in range(2):
        wd = lo[:, d * DECAY_LORA:(d + 1) * DECAY_LORA]
        ad = lo[:, 2 * DECAY_LORA + d * AAA_LORA:2 * DECAY_LORA + (d + 1) * AAA_LORA]
        z = -(w0_ref[d:d + 1, :] + _dot(jnp.tanh(wd), wup_ref[d]))
        softplus = jnp.maximum(z, 0.0) + jnp.log(1.0 + jnp.exp(-jnp.abs(z)))
        lw_o[d, 0] = -jnp.exp(-softplus - 0.5)
        a = _sigmoid(a0_ref[d:d + 1, :] + _dot(ad, aup_ref[d]))
        a_o[d, 0] = a.astype(a_o.dtype)
        kd = k * (1.0 + (a - 1.0) * ka_ref[...])
        kd_o[d, 0] = kd.astype(kd_o.dtype)
        k_sum = kd if k_sum is None else k_sum + kd
    bonus_o[0] = (_head_sum(r * k_sum * rk_ref[...], ones_ref[...]) * v).astype(bonus_o.dtype)

    cosf = jnp.concatenate([cs_ref[...]] * (W // RET_HEAD), axis=1)
    sinf = jnp.concatenate([sn_ref[...]] * (W // RET_HEAD), axis=1)
    lane = lax.broadcasted_iota(jnp.int32, (1, W), 1)
    first_half = (lane & (RET_HEAD - 1)) < RET_HEAD // 2

    def rope(t):
        rot = jnp.where(first_half, pltpu.roll(t, W - RET_HEAD // 2, 1), pltpu.roll(t, RET_HEAD // 2, 1))
        return t * cosf + rot * sinf

    rq_o[0] = rope(rqk_ref[0, :, :W]).astype(rq_o.dtype)
    rk_o[0] = (rope(rqk_ref[0, :, W:]) * (RET_HEAD ** -0.5)).astype(rk_o.dtype)

    q = hq_ref[0]
    hq_o[0] = (q * _sigmoid(q)).astype(hq_o.dtype)
    for d, f_ref in enumerate((hf0_ref, hf1_ref)):
        f = f_ref[0]
        e = jnp.exp(-jnp.abs(f))
        inv = 1.0 / (1.0 + e)
        sig_pos = jnp.where(f >= 0, inv, e * inv)
        sig_neg = jnp.where(f >= 0, e * inv, inv)
        lb = lb_ref[d:d + 1, :]
        hlf_o[d, 0] = jnp.log(lb + (1.0 - lb) * sig_pos)
        hk_o[d, 0] = ((1.0 - lb) * sig_neg).astype(hk_o.dtype)


def _prep_call(P, cols, v_first, tables, wts, n_ctx, has_vres):
    B, N, _ = P.shape
    W = BRANCH_W
    tm = ROW_TILE
    seq = N - n_ctx
    halo_per_tile = tm // GRID_W
    n_halo = N // GRID_W
    c_main, c_lora, c_vd, c_rqk, c_hq = cols

    def cur(width, col):
        return pl.BlockSpec((1, tm, width), lambda b, i: (b, i, col))

    def prev(width, col):
        return pl.BlockSpec((1, GRID_W, width), lambda b, i: (b, jnp.maximum(i * halo_per_tile - 1, 0), col))

    def nxt(width, col):
        return pl.BlockSpec((1, GRID_W, width),
                            lambda b, i: (b, jnp.minimum((i + 1) * halo_per_tile, n_halo - 1), col))

    def const(shape):
        return pl.BlockSpec(shape, lambda b, i: (0,) * len(shape))

    tok = pl.BlockSpec((1, tm, W), lambda b, i: (b, i, 0))
    tok2 = pl.BlockSpec((2, 1, tm, W), lambda b, i: (0, b, i, 0))
    cos, sin = tables
    in_specs = [cur(3 * W, c_main), prev(3 * W, c_main), nxt(3 * W, c_main),
                cur(LORA_W, c_lora), prev(LORA_W, c_lora), nxt(LORA_W, c_lora),
                cur(VDOWN_PAD, c_vd), tok, cur(2 * W, c_rqk),
                pl.BlockSpec((tm, RET_HEAD), lambda b, i: (i, 0)),
                pl.BlockSpec((tm, RET_HEAD), lambda b, i: (i, 0)),
                cur(W, c_hq), cur(W, c_hq + 1), cur(W, c_hq + 2)]
    in_specs += [const(w.shape) for w in wts]
    one = jax.ShapeDtypeStruct((B, N, W), BF16)
    two = jax.ShapeDtypeStruct((2, B, N, W), BF16)
    two_f32 = jax.ShapeDtypeStruct((2, B, N, W), F32)
    out_shape = [one, one, one, two_f32, two, two, one, one, one, one, one, two, two_f32]
    out_specs = [tok, tok, tok, tok2, tok2, tok2, tok, tok, tok, tok, tok, tok2, tok2]
    return pl.pallas_call(
        functools.partial(_prep_kernel, seq=seq, has_vres=has_vres),
        grid=(B, N // tm),
        in_specs=in_specs,
        out_specs=out_specs,
        out_shape=out_shape,
        compiler_params=pltpu.CompilerParams(
            dimension_semantics=("parallel", "parallel"), vmem_limit_bytes=VMEM_LIMIT),
        name="branch_prep",
    )(P, P, P, P, P, P, P, v_first, P, cos, sin, P, P, P, *wts)


def _merge_kernel(x_ref, g0_ref, g1_ref, g2_ref, orwf_ref, orwb_ref, g_ref, bonus_ref, oretf_ref, oretb_ref,
                  rg_ref, ohgf_ref, ohgb_ref, hg_ref,
                  lnw_ref, lnb_ref, hnw_ref, m64_ref, m128_ref, wb_ref, wo_ref, mod_ref, o_ref,
                  *, tm, n_ctx, tile_off):
    i = pl.program_id(1) + tile_off

    o = orwf_ref[0].astype(F32) + orwb_ref[0].astype(F32)
    cen = o - _head_sum(o, m64_ref[...]) * (1.0 / RWKV_HEAD)
    var = _head_sum(cen * cen, m64_ref[...]) * (1.0 / RWKV_HEAD)
    y_rw = ((cen * lax.rsqrt(var + RWKV_LN_EPS) * lnw_ref[...] + lnb_ref[...] + bonus_ref[0].astype(F32))
            * g_ref[0].astype(F32))
    o = oretf_ref[0].astype(F32) + oretb_ref[0].astype(F32)
    ms = _head_sum(o * o, m128_ref[...]) * (1.0 / RET_HEAD)
    rg = rg_ref[0]
    y_ret = o * lax.rsqrt(ms + NORM_EPS) * (rg * _sigmoid(rg))
    o = ohgf_ref[0].astype(F32) + ohgb_ref[0].astype(F32)
    ms = _head_sum(o * o, m128_ref[...]) * (1.0 / HGRN_HEAD)
    hg = hg_ref[0]
    y_hg = o * lax.rsqrt(ms + NORM_EPS) * hnw_ref[...] * (hg * _sigmoid(hg))

    merged = None
    for b, (y, gate_ref) in enumerate(((y_rw, g0_ref), (y_ret, g1_ref), (y_hg, g2_ref))):
        z = _sigmoid(gate_ref[0]) * _dot(y, wb_ref[b])
        merged = z if merged is None else merged + z
    out = _dot(merged, wo_ref[...])
    g1 = jnp.where(_is_ctx_rows(i, tm, n_ctx), mod_ref[0, 2:3, :], mod_ref[0, 8:9, :])
    o_ref[0] = x_ref[0] + g1 * out


def _merge_call(xa, P, gate_cols, o_rw, g, bonus, o_ret, o_hg, wts, modsel, n_ctx, latent_only):
    B, N, D = xa.shape
    W = BRANCH_W
    tm = ROW_TILE
    off = n_ctx // tm if latent_only else 0
    n_rows = N - off * tm
    c_gate, c_rg, c_hg = gate_cols

    def col(width, c):
        return pl.BlockSpec((1, tm, width), lambda b, i: (b, i + off, c))

    def const(shape):
        return pl.BlockSpec(shape, lambda b, i: (0,) * len(shape))

    tok = col(W, 0)
    return pl.pallas_call(
        functools.partial(_merge_kernel, tm=tm, n_ctx=n_ctx, tile_off=off),
        grid=(B, n_rows // tm),
        in_specs=[col(D, 0), col(D, c_gate), col(D, c_gate + 1), col(D, c_gate + 2),
                  tok, tok, tok, tok, tok, tok, col(W, c_rg), tok, tok, col(W, c_hg)]
                 + [const(w.shape) for w in wts]
                 + [pl.BlockSpec((1, 12, D), lambda b, i: (b, 0, 0))],
        out_specs=pl.BlockSpec((1, tm, D), lambda b, i: (b, i, 0)),
        out_shape=jax.ShapeDtypeStruct((B, n_rows, D), F32),
        compiler_params=pltpu.CompilerParams(
            dimension_semantics=("parallel", "parallel"), vmem_limit_bytes=VMEM_LIMIT),
        name="merge_out",
    )(xa, P, P, P, o_rw[0], o_rw[1], g, bonus, o_ret[0], o_ret[1], P, o_hg[0], o_hg[1], P, *wts, modsel)


def _mlp_kernel(x_ref, mod_ref, nw_ref, w1_ref, w2_ref, fw_ref, o_ref, *, tm, n_ctx, tile_off, final):
    i = pl.program_id(1) + tile_off
    is_ctx = _is_ctx_rows(i, tm, n_ctx)
    xf = x_ref[0]
    h = _modulated_norm(xf, nw_ref[...], mod_ref, is_ctx, 3)
    a = jnp.maximum(_dot(h, w1_ref[...]), 0.0)
    out = _dot(a * a, w2_ref[...])
    g2 = jnp.where(is_ctx, mod_ref[0, 5:6, :], mod_ref[0, 11:12, :])
    xn = xf + g2 * out
    if final:
        xn = xn * lax.rsqrt(jnp.mean(xn * xn, axis=-1, keepdims=True) + NORM_EPS) * fw_ref[...]
    o_ref[0] = xn


def _mlp_call(xa, modsel, nw, w1, w2, fw, n_ctx, tile_off, final):
    B, n_rows, D = xa.shape
    DF = w1.shape[1]
    tm = ROW_TILE
    return pl.pallas_call(
        functools.partial(_mlp_kernel, tm=tm, n_ctx=n_ctx, tile_off=tile_off, final=final),
        grid=(B, n_rows // tm),
        in_specs=[pl.BlockSpec((1, tm, D), lambda b, i: (b, i, 0)),
                  pl.BlockSpec((1, 12, D), lambda b, i: (b, 0, 0)),
                  pl.BlockSpec((1, D), lambda b, i: (0, 0)),
                  pl.BlockSpec((D, DF), lambda b, i: (0, 0)),
                  pl.BlockSpec((DF, D), lambda b, i: (0, 0)),
                  pl.BlockSpec((1, D), lambda b, i: (0, 0))],
        out_specs=pl.BlockSpec((1, tm, D), lambda b, i: (b, i, 0)),
        out_shape=jax.ShapeDtypeStruct((B, n_rows, D), F32),
        compiler_params=pltpu.CompilerParams(
            dimension_semantics=("parallel", "parallel"), vmem_limit_bytes=VMEM_LIMIT),
        name="mlp",
    )(xa, modsel, nw.reshape(1, D), w1, w2, fw.reshape(1, D))


def _order_masks(d, n):
    t = lax.broadcasted_iota(jnp.int32, (n, n), 0)
    s = lax.broadcasted_iota(jnp.int32, (n, n), 1)
    lead = (t - s) * (1 - 2 * d)
    return lead > 0, lead >= 0


def _scan_specs(T, W, n_ctx, N, B):
    n_ctx_blocks, n_blocks = n_ctx // T, N // T

    def block(d, j):
        if d == 0:
            return j
        return jnp.where(j < n_ctx_blocks, n_ctx_blocks - 1 - j, n_blocks - 1 + n_ctx_blocks - j)

    def shared(d, col=0):
        return pl.BlockSpec((B, T, W), lambda j: (0, block(d, j), col))

    def per_dir(d):
        return pl.BlockSpec((1, B, T, W), lambda j: (d, 0, block(d, j), 0))

    return shared, per_dir


def _scan_call(kernel_fn, name, T, specs, args, state_shape, n_ctx):
    B, N, W = args[0].shape[0], args[0].shape[1], BRANCH_W
    shared, _ = _scan_specs(T, W, n_ctx, N, B)
    out = jax.ShapeDtypeStruct((B, N, W), BF16)
    return pl.pallas_call(
        kernel_fn,
        grid=(N // T,),
        in_specs=specs,
        out_specs=[shared(0), shared(1)],
        out_shape=[out, out],
        scratch_shapes=[pltpu.VMEM((B,) + state_shape, F32)],
        compiler_params=pltpu.CompilerParams(
            dimension_semantics=("arbitrary",), vmem_limit_bytes=VMEM_LIMIT),
        name=name,
    )(*args)


def _load(ref, b):
    x = ref[b] if len(ref.shape) == 3 else ref[0, b]
    return x.astype(F32)


def _rwkv_kernel(*refs):
    sides, (of_ref, ob_ref, s_ref) = (refs[0:6], refs[6:12]), refs[12:]
    n_batch, _, n_groups = s_ref.shape[:3]

    @pl.when(pl.program_id(0) == 0)
    def _():
        s_ref[...] = jnp.zeros_like(s_ref)

    seqs = [(b, d) for b in range(n_batch) for d in range(2)]
    ins = {(b, d): tuple(_load(ref, b) for ref in sides[d]) for b, d in seqs}
    S = {(b, d, g): s_ref[b, d, g] for b, d in seqs for g in range(n_groups)}
    outs, S = _rwkv_step(seqs, ins, S, n_groups)
    for b, d in seqs:
        (of_ref, ob_ref)[d][b] = outs[b, d].astype(BF16)
    for key, s_new in S.items():
        s_ref[key] = s_new


def _rwkv_step(seqs, ins, S, n_groups):
    C = RWKV_CHUNK
    HG = 4 * RWKV_HEAD
    chains = [sq + (g,) for sq in seqs for g in range(n_groups)]

    rb = lax.broadcasted_iota(jnp.int32, (4 * C, HG), 0) // C
    lb = lax.broadcasted_iota(jnp.int32, (4 * C, HG), 1) // RWKV_HEAD
    same_head = rb == lb
    head_mask = same_head.astype(F32).astype(BF16)
    t_idx = lax.broadcasted_iota(jnp.int32, (C, HG), 0)
    s_idx = lax.broadcasted_iota(jnp.int32, (C, HG), 1) % C

    def bd(x):
        xb = x.astype(BF16)
        return jnp.concatenate([xb, xb, xb, xb], axis=0) * head_mask

    incl_c = [_order_masks(d, C)[1].astype(F32) for d in range(2)]
    G = {sq: _cumsum_rows(incl_c[sq[1]], ins[sq][3]) for sq in seqs}
    strict, incl, eye = {}, {}, {}
    for d in range(2):
        lead = (t_idx - s_idx) * (1 - 2 * d)
        strict[d], incl[d], eye[d] = lead > 0, lead >= 0, (lead == 0).astype(F32)
    q_t, r_t, p_t, k_t, p_hat, k_hat, v_g, e_end = {}, {}, {}, {}, {}, {}, {}, {}
    for sq in seqs:
        r, v, kk, lw, k, a = ins[sq]
        g_end = jnp.sum(lw, axis=0, keepdims=True)
        e_neg = jnp.exp(-G[sq])
        end = jnp.exp(g_end)
        full = dict(q=kk * jnp.exp(G[sq] - lw), r=r * jnp.exp(G[sq]), p=-(a * kk) * e_neg, k=k * e_neg, v=v)
        for g in range(n_groups):
            sl = slice(g * HG, (g + 1) * HG)
            c = sq + (g,)
            q_t[c], r_t[c], p_t[c], k_t[c], v_g[c] = (full[n][:, sl] for n in "qrpkv")
            e_end[c] = end[:, sl]
            p_hat[c], k_hat[c] = p_t[c] * e_end[c], k_t[c] * e_end[c]

    bv = {c: bd(v_g[c]) for c in chains}
    qr = {c: jnp.concatenate([q_t[c], r_t[c]], axis=0).astype(BF16) for c in chains}
    a_p = {c: _dot_nt(qr[c], bd(p_t[c])) for c in chains}
    a_k = {c: _dot_nt(qr[c], bd(k_t[c])) for c in chains}
    n_qp = {c: jnp.where(strict[c[1]], a_p[c][:C], 0.0) for c in chains}
    a_qk = {c: jnp.where(strict[c[1]], a_k[c][:C], 0.0) for c in chains}
    a_rp = {c: jnp.where(incl[c[1]], a_p[c][C:], 0.0) for c in chains}
    a_rk = {c: jnp.where(incl[c[1]], a_k[c][C:], 0.0) for c in chains}

    R = {c: _dot(n_qp[c], bd(n_qp[c])) for c in chains}
    T = {c: eye[c[1]] + n_qp[c] for c in chains}
    av = {c: _dot(a_qk[c], bv[c]) for c in chains}
    span = 4
    while span < C:
        RT = {c: _dot(jnp.concatenate([R[c], T[c]], axis=0), bd(R[c])) for c in chains}
        R = {c: RT[c][:C] for c in chains}
        T = {c: T[c] + RT[c][C:] for c in chains}
        span *= 2
    TR = {c: _dot(T[c], bd(R[c])) for c in chains}
    Tb = {c: (T[c] + TR[c]).astype(BF16) for c in chains}

    Sb = {c: S[c].astype(BF16) for c in chains}
    w = {c: _dot(Tb[c], bd(q_t[c])) for c in chains}
    u0 = {c: _dot(Tb[c], bd(av[c])) for c in chains}
    u = {c: _dot_nt(w[c], Sb[c]) + u0[c] for c in chains}
    o = {c: _dot(a_rp[c], bd(u[c])) + _dot(a_rk[c], bv[c]) + _dot_nt(r_t[c], Sb[c]) for c in chains}
    upd = {c: _dot_tn(jnp.concatenate([u[c], v_g[c]], axis=0),
                      jnp.concatenate([p_hat[c], k_hat[c]], axis=0)) for c in chains}
    S_new = {c: S[c] * e_end[c] + jnp.where(same_head, upd[c], 0.0) for c in chains}
    outs = {sq: jnp.concatenate([o[sq + (g,)] for g in range(n_groups)], axis=1) for sq in seqs}
    return outs, S_new


def _rwkv_call(r, v, kk, lw, k, a, n_ctx):
    B, N, W = r.shape
    C = RWKV_CHUNK
    assert C == RWKV_HEAD and W % (4 * RWKV_HEAD) == 0
    shared, per_dir = _scan_specs(C, W, n_ctx, N, B)
    specs = [s for d in range(2) for s in (shared(d), shared(d), shared(d), per_dir(d), per_dir(d), per_dir(d))]
    args = (r, v, kk, lw, k, a) * 2
    n_groups = W // (4 * RWKV_HEAD)
    return _scan_call(_rwkv_kernel, "rwkv7_scan", C, specs, args,
                      (2, n_groups, 4 * RWKV_HEAD, 4 * RWKV_HEAD), n_ctx)


def _ret_kernel(qf_ref, kf_ref, vf_ref, qb_ref, kb_ref, vb_ref, lg_ref, of_ref, ob_ref, s_ref):
    C = RET_CHUNK
    Dh = RET_HEAD

    @pl.when(pl.program_id(0) == 0)
    def _():
        s_ref[...] = jnp.zeros_like(s_ref)

    t = lax.broadcasted_iota(jnp.int32, (C, C), 0)
    s = lax.broadcasted_iota(jnp.int32, (C, C), 1)
    dist = jnp.abs(t - s).astype(F32)
    tcol = lax.broadcasted_iota(jnp.int32, (C, 1), 0)
    io = ((qf_ref, kf_ref, vf_ref, of_ref), (qb_ref, kb_ref, vb_ref, ob_ref))
    n_batch = s_ref.shape[0]
    n_heads = BRANCH_W // Dh
    chains = [(b, d, h) for b in range(n_batch) for d in range(2) for h in range(n_heads)]
    incl = [_order_masks(d, C)[1] for d in range(2)]
    n_t = [(tcol + 1 if d == 0 else C - tcol).astype(F32) for d in range(2)]

    def head(ref, b, h):
        return ref[b, :, h * Dh:(h + 1) * Dh].astype(F32)

    scores = {(b, d, h): _dot_nt(head(io[d][0], b, h), head(io[d][1], b, h)) for b, d, h in chains}
    outs, H = {}, {}
    for b, d, h in chains:
        lg = lg_ref[d, :, h * Dh:h * Dh + C]
        lgd = lg_ref[d, :, h * Dh:(h + 1) * Dh]
        A = scores[b, d, h] * jnp.where(incl[d], jnp.exp(lg * dist), 0.0)
        H[b, d, h] = s_ref[b, d, h]
        outs[b, d, h] = (_dot(A, head(io[d][2], b, h))
                         + _dot(head(io[d][0], b, h) * jnp.exp(lgd * n_t[d]), H[b, d, h]))
    for b, d, h in chains:
        lgd = lg_ref[d, :, h * Dh:(h + 1) * Dh]
        k_hat = head(io[d][1], b, h) * jnp.exp(lgd * (C - n_t[d]))
        s_ref[b, d, h] = H[b, d, h] * jnp.exp(lgd * C) + _dot_tn(k_hat, head(io[d][2], b, h))
    for b in range(n_batch):
        for d in range(2):
            io[d][3][b] = jnp.concatenate([outs[b, d, h] for h in range(n_heads)], axis=1).astype(BF16)


def _ret_call(q, k, v, v_col, lg, n_ctx):
    B, N, W = q.shape
    C = RET_CHUNK
    assert C <= RET_HEAD
    shared, _ = _scan_specs(C, W, n_ctx, N, B)
    specs = [s for d in range(2) for s in (shared(d), shared(d), shared(d, v_col))]
    specs.append(pl.BlockSpec((2, 1, W), lambda j: (0, 0, 0)))
    return _scan_call(_ret_kernel, "retention_scan", C, specs, (q, k, v, q, k, v, lg),
                      (2, W // RET_HEAD, RET_HEAD, RET_HEAD), n_ctx)


def _hgrn_kernel(qf_ref, vf_ref, kf_ref, lff_ref, qb_ref, vb_ref, kb_ref, lfb_ref, of_ref, ob_ref, s_ref):
    C = HGRN_CHUNK
    n_batch, _, n_heads = s_ref.shape[:3]

    @pl.when(pl.program_id(0) == 0)
    def _():
        s_ref[...] = jnp.zeros_like(s_ref)

    io = ((qf_ref, vf_ref, kf_ref, lff_ref, of_ref), (qb_ref, vb_ref, kb_ref, lfb_ref, ob_ref))
    n_chunks = HGRN_STEP // C
    order = [list(range(n_chunks)), list(reversed(range(n_chunks)))]
    seqs = [(b, d) for b in range(n_batch) for d in range(2)]
    blocks = [sq + (c,) for sq in seqs for c in range(n_chunks)]
    incl_c = [_order_masks(d, C)[1].astype(F32) for d in range(2)]

    def rows(ref, b, c):
        x = ref[b, c * C:(c + 1) * C, :] if len(ref.shape) == 3 else ref[0, b, c * C:(c + 1) * C, :]
        return x.astype(F32)

    G = {(b, d, c): _cumsum_rows(incl_c[d], rows(io[d][3], b, c)) for b, d, c in blocks}
    pre = {}
    for b, d, c in blocks:
        q_ref, v_ref, k_ref, lf_ref, _ = io[d]
        pre[b, d, c] = _hgrn_intra(d, rows(q_ref, b, c), rows(k_ref, b, c), rows(v_ref, b, c),
                                   rows(lf_ref, b, c), G[b, d, c])
    heads = range(n_heads)
    intra = {blk + (h,): _dot(pre[blk]["A"][h], pre[blk]["v"][h]) for blk in blocks for h in heads}

    S = {sq + (h,): s_ref[sq + (h,)] for sq in seqs for h in heads}
    for i in range(n_chunks):
        inter, S_next = {}, {}
        for b, d in seqs:
            blk = pre[b, d, order[d][i]]
            for h in heads:
                inter[b, d, h] = _dot_nt(blk["q_full"][h], S[b, d, h])
                S_next[b, d, h] = S[b, d, h] * blk["e_end"][h] + _dot_tn(blk["v"][h], blk["k_hat"][h])
        for b, d in seqs:
            c = order[d][i]
            io[d][4][b, c * C:(c + 1) * C, :] = jnp.concatenate(
                [intra[b, d, c, h] + inter[b, d, h] for h in heads], axis=1).astype(BF16)
        S = S_next
    for key, s_new in S.items():
        s_ref[key] = s_new


def _hgrn_intra(d, q, k, v, lf, G):
    C, SUB = HGRN_CHUNK, HGRN_SUB
    Dh = HGRN_HEAD
    n_sub = C // SUB
    _, incl = _order_masks(d, C)
    g_end = jnp.sum(lf, axis=0, keepdims=True)
    W = q.shape[-1]
    anchors = [G[I * SUB + SUB // 2:I * SUB + SUB // 2 + 1, :] for I in range(n_sub)]
    g_anchor = jnp.concatenate([jnp.broadcast_to(a, (SUB, W)) for a in anchors], axis=0)
    q_a = q * jnp.exp(G - g_anchor)
    q_full = q * jnp.exp(G)
    k_hat = k * jnp.exp(g_end - G)
    sub_of_row = lax.broadcasted_iota(jnp.int32, (C, 1), 0) // SUB
    k_anch = []
    for I in range(n_sub):
        visible = (I - sub_of_row) * (1 - 2 * d) >= 0
        k_anch.append(k * jnp.exp(jnp.where(visible, anchors[I] - G, 0.0)))
    e_end = jnp.exp(g_end)
    out = dict(A=[], v=[], q_full=[], k_hat=[], e_end=[])
    for h in range(W // Dh):
        sl = slice(h * Dh, (h + 1) * Dh)
        A = jnp.concatenate([_dot_nt(q_a[I * SUB:(I + 1) * SUB, sl], k_anch[I][:, sl]) for I in range(n_sub)],
                            axis=0)
        out["A"].append(jnp.where(incl, A, 0.0))
        out["v"].append(v[:, sl])
        out["q_full"].append(q_full[:, sl])
        out["k_hat"].append(k_hat[:, sl])
        out["e_end"].append(e_end[:, sl])
    return out


def _hgrn_call(q, v, v_col, k, lf, n_ctx):
    B, N, W = q.shape
    T = HGRN_STEP
    shared, per_dir = _scan_specs(T, W, n_ctx, N, B)
    specs = [s for d in range(2) for s in (shared(d), shared(d, v_col), per_dir(d), per_dir(d))]
    return _scan_call(_hgrn_kernel, "hgrn2_scan", T, specs, (q, v, k, lf) * 2,
                      (2, W // HGRN_HEAD, HGRN_HEAD, HGRN_HEAD), n_ctx)


def _rope_tables(n, head):
    half = head // 2
    inv_freq = ROPE_BASE ** (-jnp.arange(half, dtype=F32) / half)
    ang = jnp.arange(n).astype(F32)[:, None] * inv_freq[None, :]
    cos, sin = jnp.cos(ang), jnp.sin(ang)
    return jnp.concatenate([cos, cos], axis=1), jnp.concatenate([-sin, sin], axis=1)


def _block_diag_ones(width, head):
    idx = jnp.arange(width) // head
    return (idx[:, None] == idx[None, :]).astype(BF16)


def _permute_w_in(w_in_l, v_down_l):
    D = w_in_l.shape[0]
    n_rw, n_ret, n_hg = 3 * BRANCH_W + LORA_W, 4 * BRANCH_W, 5 * BRANCH_W
    rw = w_in_l[:, :n_rw]
    ret = w_in_l[:, n_rw:n_rw + n_ret]
    hg = w_in_l[:, n_rw + n_ret:n_rw + n_ret + n_hg]
    gate = w_in_l[:, n_rw + n_ret + n_hg:]
    extra = jnp.zeros((D, VDOWN_PAD), F32)
    if v_down_l is not None:
        extra = extra.at[:, :v_down_l.shape[1]].set(v_down_l)
    return jnp.concatenate([gate, rw, extra, ret, hg], axis=1).astype(BF16)


def kernel(x, c, ctx, c_ctx, ada_w, ada_b, norm1_w, norm2_w, w_in, rwkv_mu, rwkv_w0, rwkv_w_up, rwkv_a0, rwkv_a_up, rwkv_g_up, rwkv_k_k, rwkv_k_a, rwkv_r_k, rwkv_lnx_w, rwkv_lnx_b, rwkv_v0, rwkv_v_down, rwkv_v_up, ret_decay, hgrn_lb, hgrn_norm_w, w_branch, w_out, mlp_w1, mlp_w2, final_norm_w):
    B, SEQ, D = x.shape
    n_ctx = ctx.shape[1]
    N = n_ctx + SEQ
    L = ada_w.shape[0]
    W = BRANCH_W
    assert n_ctx == ROW_TILE and SEQ % ROW_TILE == 0 and D == 2 * W
    o_rw = 3 * D
    o_lora = o_rw + 3 * W
    o_vd = o_lora + LORA_W
    o_ret = o_vd + VDOWN_PAD
    o_hg = o_ret + 4 * W
    prep_cols = (o_rw // (3 * W), o_lora // LORA_W, o_vd // VDOWN_PAD, o_ret // (2 * W), o_hg // W)
    gate_cols = (0, (o_ret + 3 * W) // W, (o_hg + 4 * W) // W)
    assert o_rw % (3 * W) == 0 and o_lora % LORA_W == 0 and o_ret % (2 * W) == 0

    xa = jnp.concatenate([ctx, x], axis=1)
    cvec = jnp.concatenate([c, c_ctx[None], jnp.zeros((8 - B - 1, D), F32)], axis=0)
    mod = _ada_call(cvec, ada_w, ada_b)
    sm = jax.nn.softmax(hgrn_lb.astype(F32), axis=0)
    hgrn_lower = jnp.cumsum(sm, axis=0) - sm[0:1]
    tables = _rope_tables(N, RET_HEAD)
    ones64 = _block_diag_ones(W, RWKV_HEAD)
    ones128 = _block_diag_ones(W, RET_HEAD)
    row = lambda t: t.reshape(1, -1)
    v_first = None
    out = None

    for l in range(L):
        last = l == L - 1
        mod_c = jnp.broadcast_to(mod[l, B].reshape(1, 6, D), (B, 6, D))
        modsel = jnp.concatenate([mod_c, mod[l, :B].reshape(B, 6, D)], axis=1)
        w_l = _permute_w_in(w_in[l], None if l == 0 else rwkv_v_down[l - 1])
        P = _proj_call(xa, modsel, norm1_w[l], w_l, n_ctx)

        has_vres = l > 0
        if has_vres:
            v0 = row(rwkv_v0[l - 1])
            v_up = jnp.zeros((VDOWN_PAD, W), F32).at[:rwkv_v_up.shape[1]].set(rwkv_v_up[l - 1]).astype(BF16)
        else:
            v0 = jnp.zeros((1, W), F32)
            v_up = jnp.zeros((VDOWN_PAD, W), BF16)
        prep_w = (row(rwkv_mu[l, :3 * W]), row(rwkv_mu[l, 3 * W:]),
                  rwkv_w_up[l].astype(BF16), rwkv_w0[l], rwkv_a_up[l].astype(BF16), rwkv_a0[l],
                  rwkv_g_up[l].astype(BF16), row(rwkv_k_k[l]), row(rwkv_k_a[l]), row(rwkv_r_k[l]),
                  v0, v_up, ones64, hgrn_lower[l])
        (r, v, kk, lw, kd, a, g, bonus, rq, rk, hq, hk, hlf) = _prep_call(
            P, prep_cols, P if v_first is None else v_first, tables, prep_w, n_ctx, has_vres)
        if v_first is None:
            v_first = v

        o_rw_dirs = _rwkv_call(r, v, kk, lw, kd, a, n_ctx)
        log_gamma = -jnp.exp(ret_decay[l].astype(F32))
        lg = jnp.repeat(log_gamma, RET_HEAD, axis=-1).reshape(2, 1, W)
        o_ret_dirs = _ret_call(rq, rk, P, (o_ret + 2 * W) // W, lg, n_ctx)
        o_hg_dirs = _hgrn_call(hq, P, (o_hg + 3 * W) // W, hk, hlf, n_ctx)

        merge_w = (row(rwkv_lnx_w[l]), row(rwkv_lnx_b[l]), row(hgrn_norm_w[l]), ones64, ones128,
                   w_branch[l].astype(BF16), w_out[l].astype(BF16))
        xm = _merge_call(xa, P, gate_cols, o_rw_dirs, g, bonus, o_ret_dirs, o_hg_dirs, merge_w, modsel,
                         n_ctx, last)
        xa = _mlp_call(xm, modsel, norm2_w[l], mlp_w1[l].astype(BF16), mlp_w2[l].astype(BF16),
                       final_norm_w, n_ctx, (n_ctx // ROW_TILE) if last else 0, last)
        out = xa
    return out
```

```python
import functools

import jax
import jax.numpy as jnp
from jax import lax
from jax.experimental import pallas as pl
from jax.experimental.pallas import tpu as pltpu

F32 = jnp.float32
BF16 = jnp.bfloat16

NORM_EPS = 1e-6
RWKV_LN_EPS = 64e-5
L2_EPS = 1e-12
ROPE_BASE = 10000.0
GRID_W = 64

BRANCH_W = 512
RWKV_HEAD = 64
RET_HEAD = 128
HGRN_HEAD = 128
DECAY_LORA = 64
AAA_LORA = 64
GATE_LORA = 128
LORA_W = 2 * DECAY_LORA + 2 * AAA_LORA + GATE_LORA
VDOWN_PAD = 128

RWKV_CHUNK = 64
HGRN_CHUNK = 64
HGRN_SUB = 32
HGRN_STEP = 128
RET_CHUNK = 128
ROW_TILE = 256

VMEM_LIMIT = 56 * 1024 * 1024

D_MODEL = 2 * BRANCH_W
MAIN_COLS = dict(gate=0, rwkv=3 * D_MODEL, ret=3 * D_MODEL + 3 * BRANCH_W, hgrn_q=3 * D_MODEL + 7 * BRANCH_W,
                 hgrn_i=3 * D_MODEL + 8 * BRANCH_W, hgrn_g=3 * D_MODEL + 9 * BRANCH_W)
MAIN_WIDTH = 3 * D_MODEL + 10 * BRANCH_W
DECAY_COLS = dict(lora=0, v_down=LORA_W, hgrn_f=LORA_W + VDOWN_PAD)
DECAY_WIDTH = LORA_W + VDOWN_PAD + 2 * BRANCH_W


def _sigmoid(x):
    return 1.0 / (1.0 + jnp.exp(-x))


def _dot(a, b):
    return jnp.dot(a.astype(BF16), b.astype(BF16), preferred_element_type=F32)


def _dot_nt(a, b):
    return lax.dot_general(a.astype(BF16), b.astype(BF16), (((1,), (1,)), ((), ())),
                           preferred_element_type=F32)


def _dot_tn(a, b):
    return lax.dot_general(a.astype(BF16), b.astype(BF16), (((0,), (0,)), ((), ())),
                           preferred_element_type=F32)


def _cumsum_rows(mask_f32, x):
    m = mask_f32.astype(BF16)
    hi = x.astype(BF16)
    lo = (x - hi.astype(F32)).astype(BF16)
    return jnp.dot(m, hi, preferred_element_type=F32) + jnp.dot(m, lo, preferred_element_type=F32)


def _head_sum(x, ones_bd, split=True):
    hi = x.astype(BF16)
    out = jnp.dot(hi, ones_bd, preferred_element_type=F32)
    if split:
        lo = (x - hi.astype(F32)).astype(BF16)
        out = out + jnp.dot(lo, ones_bd, preferred_element_type=F32)
    return out


def _ada_kernel(c_ref, w_ref, b_ref, o_ref):
    cv = c_ref[...]
    s = cv * _sigmoid(cv)
    o_ref[0] = _dot(s, w_ref[0]) + b_ref[0]


def _ada_call(cvec, ada_w, ada_b):
    L, D, D6 = ada_w.shape
    tn = D6 // 4
    return pl.pallas_call(
        _ada_kernel,
        grid=(L, D6 // tn),
        in_specs=[pl.BlockSpec((8, D), lambda l, j: (0, 0)),
                  pl.BlockSpec((1, D, tn), lambda l, j: (l, 0, j)),
                  pl.BlockSpec((1, 1, tn), lambda l, j: (l, 0, j))],
        out_specs=pl.BlockSpec((1, 8, tn), lambda l, j: (l, 0, j)),
        out_shape=jax.ShapeDtypeStruct((L, 8, D6), F32),
        compiler_params=pltpu.CompilerParams(vmem_limit_bytes=VMEM_LIMIT),
        name="ada_mod",
    )(cvec, ada_w, ada_b.reshape(L, 1, D6))


def _modulated_norm(xf, nw, mod_ref, is_ctx, which):
    y = xf * lax.rsqrt(jnp.mean(xf * xf, axis=-1, keepdims=True) + NORM_EPS) * nw
    sh = jnp.where(is_ctx, mod_ref[0, which:which + 1, :], mod_ref[0, 6 + which:7 + which, :])
    sc = jnp.where(is_ctx, mod_ref[0, which + 1:which + 2, :], mod_ref[0, 7 + which:8 + which, :])
    return y * (1.0 + sc) + sh


def _is_ctx_rows(tile_idx, tm, n_ctx):
    row = tile_idx * tm + lax.broadcasted_iota(jnp.int32, (tm, 1), 0)
    return row < n_ctx


def _proj_kernel(x_ref, mod_ref, nw_ref, w_ref, o_ref, h_ref, *, tm, n_ctx):
    i = pl.program_id(1)

    @pl.when(pl.program_id(2) == 0)
    def _():
        h = _modulated_norm(x_ref[0], nw_ref[...], mod_ref, _is_ctx_rows(i, tm, n_ctx), 0)
        h_ref[...] = h.astype(BF16)

    o_ref[0] = jnp.dot(h_ref[...], w_ref[...], preferred_element_type=F32).astype(o_ref.dtype)


def _proj_call(xa, modsel, nw, w, n_ctx, out_dtype, n_tiles, tm=768):
    B, N, D = xa.shape
    NP = w.shape[1]
    tn = NP // n_tiles
    return pl.pallas_call(
        functools.partial(_proj_kernel, tm=tm, n_ctx=n_ctx),
        grid=(B, N // tm, n_tiles),
        in_specs=[pl.BlockSpec((1, tm, D), lambda b, i, j: (b, i, 0)),
                  pl.BlockSpec((1, 12, D), lambda b, i, j: (b, 0, 0)),
                  pl.BlockSpec((1, D), lambda b, i, j: (0, 0)),
                  pl.BlockSpec((D, tn), lambda b, i, j: (0, j))],
        out_specs=pl.BlockSpec((1, tm, tn), lambda b, i, j: (b, i, j)),
        out_shape=jax.ShapeDtypeStruct((B, N, NP), out_dtype),
        scratch_shapes=[pltpu.VMEM((tm, D), BF16)],
        compiler_params=pltpu.CompilerParams(
            dimension_semantics=("parallel", "parallel", "arbitrary"),
            vmem_limit_bytes=VMEM_LIMIT),
        name="in_proj",
    )(xa, modsel, nw.reshape(1, D), w)


def _shift_mix(cur, prev, nxt, mu, i, seq):
    tm = cur.shape[0]
    row = lax.broadcasted_iota(jnp.int32, (tm, 1), 0)
    is_ctx = i == 0
    period_mask = jnp.where(is_ctx, tm - 1, GRID_W - 1)
    col = row & period_mask
    cur = cur.astype(F32)
    ext = jnp.concatenate([prev.astype(F32), cur, nxt.astype(F32)], axis=0)
    up = ext[0:tm]
    down = ext[2 * GRID_W:2 * GRID_W + tm]
    lat_row = (i - 1) * tm + row
    up = jnp.where(lat_row >= GRID_W, up, 0.0)
    down = jnp.where(lat_row < seq - GRID_W, down, 0.0)
    left = jnp.where(col != 0, pltpu.roll(cur, 1, 0), 0.0)
    right = jnp.where(col != period_mask, pltpu.roll(cur, tm - 1, 0), 0.0)
    w_lr = jnp.where(is_ctx, 0.5, 0.25)
    w_ud = jnp.where(is_ctx, 0.0, 0.25)
    shifted = w_lr * (left + right) + w_ud * (up + down)
    return cur + mu * (shifted - cur)


def _prep_kernel(pc_ref, pp_ref, pn_ref, lc_ref, lp_ref, ln_ref, hv_ref, vf_ref, rq_ref, rk_ref, cs_ref, sn_ref,
                 hq_ref, hf0_ref, hf1_ref,
                 mum_ref, mul_ref, wup_ref, w0_ref, aup_ref, a0_ref, gup_ref, kk_w_ref, ka_ref, r_k_ref,
                 v0_ref, vup_ref, ones_ref, lb_ref,
                 r_o, v_o, kk_o, lw_o, kd_o, a_o, g_o, bonus_o, rq_o, rk_o, hq_o, hk_o, hlf_o,
                 *, seq, has_vres):
    i = pl.program_id(1)
    W = BRANCH_W

    p = _shift_mix(pc_ref[0], pp_ref[0], pn_ref[0], mum_ref[...], i, seq)
    lo = _shift_mix(lc_ref[0], lp_ref[0], ln_ref[0], mul_ref[...], i, seq)
    r, k, v = p[:, :W], p[:, W:2 * W], p[:, 2 * W:]
    if has_vres:
        mix = _sigmoid(v0_ref[...] + _dot(hv_ref[0], vup_ref[...]))
        v = v + (vf_ref[0].astype(F32) - v) * mix
    r_o[0] = r.astype(r_o.dtype)
    v_o[0] = v.astype(v_o.dtype)
    gd = lo[:, 2 * DECAY_LORA + 2 * AAA_LORA:]
    g_o[0] = _dot(_sigmoid(gd), gup_ref[...]).astype(g_o.dtype)
    kx = k * kk_w_ref[...]
    kk_o[0] = (kx * lax.rsqrt(_head_sum(kx * kx, ones_ref[...]) + L2_EPS)).astype(kk_o.dtype)
    k_sum = None
    for d in range(2):
        wd = lo[:, d * DECAY_LORA:(d + 1) * DECAY_LORA]
        ad = lo[:, 2 * DECAY_LORA + d * AAA_LORA:2 * DECAY_LORA + (d + 1) * AAA_LORA]
        z = -(w0_ref[d:d + 1, :] + _dot(jnp.tanh(wd), wup_ref[d]))
        softplus = jnp.maximum(z, 0.0) + jnp.log(1.0 + jnp.exp(-jnp.abs(z)))
        lw_o[d, 0] = -jnp.exp(-softplus - 0.5)
        a = _sigmoid(a0_ref[d:d + 1, :] + _dot(ad, aup_ref[d]))
        a_o[d, 0] = a.astype(a_o.dtype)
        kd = k * (1.0 + (a - 1.0) * ka_ref[...])
        kd_o[d, 0] = kd.astype(kd_o.dtype)
        k_sum = kd if k_sum is None else k_sum + kd
    bonus_o[0] = (_head_sum(r * k_sum * r_k_ref[...], ones_ref[...]) * v).astype(bonus_o.dtype)

    cosf = jnp.concatenate([cs_ref[...]] * (W // RET_HEAD), axis=1)
    sinf = jnp.concatenate([sn_ref[...]] * (W // RET_HEAD), axis=1)
    lane = lax.broadcasted_iota(jnp.int32, (1, W), 1)
    first_half = (lane & (RET_HEAD - 1)) < RET_HEAD // 2

    def rope(t):
        rot = jnp.where(first_half, pltpu.roll(t, W - RET_HEAD // 2, 1), pltpu.roll(t, RET_HEAD // 2, 1))
        return t * cosf + rot * sinf

    rq_o[0] = rope(rq_ref[0].astype(F32)).astype(rq_o.dtype)
    rk_o[0] = (rope(rk_ref[0].astype(F32)) * (RET_HEAD ** -0.5)).astype(rk_o.dtype)

    q = hq_ref[0].astype(F32)
    hq_o[0] = (q * _sigmoid(q)).astype(hq_o.dtype)
    for d, f_ref in enumerate((hf0_ref, hf1_ref)):
        f = f_ref[0]
        e = jnp.exp(-jnp.abs(f))
        inv = 1.0 / (1.0 + e)
        sig_pos = jnp.where(f >= 0, inv, e * inv)
        sig_neg = jnp.where(f >= 0, e * inv, inv)
        lb = lb_ref[d:d + 1, :]
        hlf_o[d, 0] = jnp.log(lb + (1.0 - lb) * sig_pos)
        hk_o[d, 0] = ((1.0 - lb) * sig_neg).astype(hk_o.dtype)


def _prep_call(P, Pd, v_first, tables, wts, n_ctx, has_vres):
    B, N, _ = P.shape
    W = BRANCH_W
    tm = ROW_TILE
    seq = N - n_ctx
    halo_per_tile = tm // GRID_W
    n_halo = N // GRID_W
    c_main = MAIN_COLS["rwkv"] // (3 * W)
    c_rq = MAIN_COLS["ret"] // W
    c_hq = MAIN_COLS["hgrn_q"] // W
    c_lora = DECAY_COLS["lora"] // LORA_W
    c_vd = DECAY_COLS["v_down"] // VDOWN_PAD
    c_hf = DECAY_COLS["hgrn_f"] // W

    def cur(width, col):
        return pl.BlockSpec((1, tm, width), lambda b, i: (b, i, col))

    def prev(width, col):
        return pl.BlockSpec((1, GRID_W, width), lambda b, i: (b, jnp.maximum(i * halo_per_tile - 1, 0), col))

    def nxt(width, col):
        return pl.BlockSpec((1, GRID_W, width),
                            lambda b, i: (b, jnp.minimum((i + 1) * halo_per_tile, n_halo - 1), col))

    def const(shape):
        return pl.BlockSpec(shape, lambda b, i: (0,) * len(shape))

    tok = pl.BlockSpec((1, tm, W), lambda b, i: (b, i, 0))
    tok2 = pl.BlockSpec((2, 1, tm, W), lambda b, i: (0, b, i, 0))
    cos, sin = tables
    in_specs = [cur(3 * W, c_main), prev(3 * W, c_main), nxt(3 * W, c_main),
                cur(LORA_W, c_lora), prev(LORA_W, c_lora), nxt(LORA_W, c_lora),
                cur(VDOWN_PAD, c_vd), tok, cur(W, c_rq), cur(W, c_rq + 1),
                pl.BlockSpec((tm, RET_HEAD), lambda b, i: (i, 0)),
                pl.BlockSpec((tm, RET_HEAD), lambda b, i: (i, 0)),
                cur(W, c_hq), cur(W, c_hf), cur(W, c_hf + 1)]
    in_specs += [const(w.shape) for w in wts]
    one = jax.ShapeDtypeStruct((B, N, W), BF16)
    two = jax.ShapeDtypeStruct((2, B, N, W), BF16)
    two_f32 = jax.ShapeDtypeStruct((2, B, N, W), F32)
    out_shape = [one, one, one, two_f32, two, two, one, one, one, one, one, two, two_f32]
    out_specs = [tok, tok, tok, tok2, tok2, tok2, tok, tok, tok, tok, tok, tok2, tok2]
    return pl.pallas_call(
        functools.partial(_prep_kernel, seq=seq, has_vres=has_vres),
        grid=(B, N // tm),
        in_specs=in_specs,
        out_specs=out_specs,
        out_shape=out_shape,
        compiler_params=pltpu.CompilerParams(
            dimension_semantics=("parallel", "parallel"), vmem_limit_bytes=VMEM_LIMIT),
        name="branch_prep",
    )(P, P, P, Pd, Pd, Pd, Pd, v_first, P, P, cos, sin, P, Pd, Pd, *wts)


def _merge_kernel(x_ref, g0_ref, g1_ref, g2_ref, orwf_ref, orwb_ref, g_ref, bonus_ref, oretf_ref, oretb_ref,
                  rg_ref, ohgf_ref, ohgb_ref, hg_ref,
                  lnw_ref, lnb_ref, hnw_ref, m64_ref, m128_ref, wb_ref, wo_ref, mod_ref, o_ref,
                  *, tm, n_ctx, tile_off):
    i = pl.program_id(1) + tile_off

    o = orwf_ref[0].astype(F32) + orwb_ref[0].astype(F32)
    cen = o - _head_sum(o, m64_ref[...]) * (1.0 / RWKV_HEAD)
    var = _head_sum(cen * cen, m64_ref[...], split=False) * (1.0 / RWKV_HEAD)
    y_rw = ((cen * lax.rsqrt(var + RWKV_LN_EPS) * lnw_ref[...] + lnb_ref[...] + bonus_ref[0].astype(F32))
            * g_ref[0].astype(F32))
    o = oretf_ref[0].astype(F32) + oretb_ref[0].astype(F32)
    ms = _head_sum(o * o, m128_ref[...], split=False) * (1.0 / RET_HEAD)
    rg = rg_ref[0].astype(F32)
    y_ret = o * lax.rsqrt(ms + NORM_EPS) * (rg * _sigmoid(rg))
    o = ohgf_ref[0].astype(F32) + ohgb_ref[0].astype(F32)
    ms = _head_sum(o * o, m128_ref[...], split=False) * (1.0 / HGRN_HEAD)
    hg = hg_ref[0].astype(F32)
    y_hg = o * lax.rsqrt(ms + NORM_EPS) * hnw_ref[...] * (hg * _sigmoid(hg))

    merged = None
    for b, (y, gate_ref) in enumerate(((y_rw, g0_ref), (y_ret, g1_ref), (y_hg, g2_ref))):
        z = _sigmoid(gate_ref[0].astype(F32)) * _dot(y, wb_ref[b])
        merged = z if merged is None else merged + z
    out = _dot(merged, wo_ref[...])
    g1 = jnp.where(_is_ctx_rows(i, tm, n_ctx), mod_ref[0, 2:3, :], mod_ref[0, 8:9, :])
    o_ref[0] = x_ref[0] + g1 * out


def _merge_call(xa, P, o_rw, g, bonus, o_ret, o_hg, wts, modsel, n_ctx, latent_only):
    B, N, D = xa.shape
    W = BRANCH_W
    tm = ROW_TILE
    off = n_ctx // tm if latent_only else 0
    n_rows = N - off * tm
    c_gate = MAIN_COLS["gate"] // D
    c_rg = (MAIN_COLS["ret"] + 3 * W) // W
    c_hg = MAIN_COLS["hgrn_g"] // W

    def col(width, c):
        return pl.BlockSpec((1, tm, width), lambda b, i: (b, i + off, c))

    def const(shape):
        return pl.BlockSpec(shape, lambda b, i: (0,) * len(shape))

    tok = col(W, 0)
    return pl.pallas_call(
        functools.partial(_merge_kernel, tm=tm, n_ctx=n_ctx, tile_off=off),
        grid=(B, n_rows // tm),
        in_specs=[col(D, 0), col(D, c_gate), col(D, c_gate + 1), col(D, c_gate + 2),
                  tok, tok, tok, tok, tok, tok, col(W, c_rg), tok, tok, col(W, c_hg)]
                 + [const(w.shape) for w in wts]
                 + [pl.BlockSpec((1, 12, D), lambda b, i: (b, 0, 0))],
        out_specs=pl.BlockSpec((1, tm, D), lambda b, i: (b, i, 0)),
        out_shape=jax.ShapeDtypeStruct((B, n_rows, D), F32),
        compiler_params=pltpu.CompilerParams(
            dimension_semantics=("parallel", "parallel"), vmem_limit_bytes=VMEM_LIMIT),
        name="merge_out",
    )(xa, P, P, P, o_rw[0], o_rw[1], g, bonus, o_ret[0], o_ret[1], P, o_hg[0], o_hg[1], P, *wts, modsel)


def _mlp_kernel(x_ref, mod_ref, nw_ref, w1_ref, w2_ref, fw_ref, o_ref, *, tm, n_ctx, tile_off, final):
    i = pl.program_id(1) + tile_off
    is_ctx = _is_ctx_rows(i, tm, n_ctx)
    xf = x_ref[0]
    h = _modulated_norm(xf, nw_ref[...], mod_ref, is_ctx, 3)
    a = jnp.maximum(_dot(h, w1_ref[...]), 0.0)
    out = _dot(a * a, w2_ref[...])
    g2 = jnp.where(is_ctx, mod_ref[0, 5:6, :], mod_ref[0, 11:12, :])
    xn = xf + g2 * out
    if final:
        xn = xn * lax.rsqrt(jnp.mean(xn * xn, axis=-1, keepdims=True) + NORM_EPS) * fw_ref[...]
    o_ref[0] = xn


def _mlp_call(xa, modsel, nw, w1, w2, fw, n_ctx, tile_off, final):
    B, n_rows, D = xa.shape
    DF = w1.shape[1]
    tm = ROW_TILE
    return pl.pallas_call(
        functools.partial(_mlp_kernel, tm=tm, n_ctx=n_ctx, tile_off=tile_off, final=final),
        grid=(B, n_rows // tm),
        in_specs=[pl.BlockSpec((1, tm, D), lambda b, i: (b, i, 0)),
                  pl.BlockSpec((1, 12, D), lambda b, i: (b, 0, 0)),
                  pl.BlockSpec((1, D), lambda b, i: (0, 0)),
                  pl.BlockSpec((D, DF), lambda b, i: (0, 0)),
                  pl.BlockSpec((DF, D), lambda b, i: (0, 0)),
                  pl.BlockSpec((1, D), lambda b, i: (0, 0))],
        out_specs=pl.BlockSpec((1, tm, D), lambda b, i: (b, i, 0)),
        out_shape=jax.ShapeDtypeStruct((B, n_rows, D), F32),
        compiler_params=pltpu.CompilerParams(
            dimension_semantics=("parallel", "parallel"), vmem_limit_bytes=VMEM_LIMIT),
        name="mlp",
    )(xa, modsel, nw.reshape(1, D), w1, w2, fw.reshape(1, D))


def _order_masks(d, n):
    t = lax.broadcasted_iota(jnp.int32, (n, n), 0)
    s = lax.broadcasted_iota(jnp.int32, (n, n), 1)
    lead = (t - s) * (1 - 2 * d)
    return lead > 0, lead >= 0


def _scan_specs(T, W, n_ctx, N, B):
    n_ctx_blocks, n_blocks = n_ctx // T, N // T

    def block(d, j):
        if d == 0:
            return j
        return jnp.where(j < n_ctx_blocks, n_ctx_blocks - 1 - j, n_blocks - 1 + n_ctx_blocks - j)

    def shared(d, col=0):
        return pl.BlockSpec((B, T, W), lambda j: (0, block(d, j), col))

    def per_dir(d):
        return pl.BlockSpec((1, B, T, W), lambda j: (d, 0, block(d, j), 0))

    return shared, per_dir


def _scan_call(kernel_fn, name, T, specs, args, state_shape, n_ctx):
    B, N, W = args[0].shape[0], args[0].shape[1], BRANCH_W
    shared, _ = _scan_specs(T, W, n_ctx, N, B)
    out = jax.ShapeDtypeStruct((B, N, W), BF16)
    return pl.pallas_call(
        kernel_fn,
        grid=(N // T,),
        in_specs=specs,
        out_specs=[shared(0), shared(1)],
        out_shape=[out, out],
        scratch_shapes=[pltpu.VMEM((B,) + state_shape, F32)],
        compiler_params=pltpu.CompilerParams(
            dimension_semantics=("arbitrary",), vmem_limit_bytes=VMEM_LIMIT),
        name=name,
    )(*args)


def _load(ref, b):
    x = ref[b] if len(ref.shape) == 3 else ref[0, b]
    return x.astype(F32)


def _rwkv_kernel(*refs):
    sides, (of_ref, ob_ref, s_ref) = (refs[0:6], refs[6:12]), refs[12:]
    n_batch, _, n_groups = s_ref.shape[:3]

    @pl.when(pl.program_id(0) == 0)
    def _():
        s_ref[...] = jnp.zeros_like(s_ref)

    seqs = [(b, d) for b in range(n_batch) for d in range(2)]
    ins = {(b, d): tuple(_load(ref, b) for ref in sides[d]) for b, d in seqs}
    S = {(b, d, g): s_ref[b, d, g] for b, d in seqs for g in range(n_groups)}
    outs, S = _rwkv_step(seqs, ins, S, n_groups)
    for b, d in seqs:
        (of_ref, ob_ref)[d][b] = outs[b, d].astype(BF16)
    for key, s_new in S.items():
        s_ref[key] = s_new


def _rwkv_step(seqs, ins, S, n_groups):
    C = RWKV_CHUNK
    HG = 4 * RWKV_HEAD
    chains = [sq + (g,) for sq in seqs for g in range(n_groups)]

    rb = lax.broadcasted_iota(jnp.int32, (4 * C, HG), 0) // C
    lb = lax.broadcasted_iota(jnp.int32, (4 * C, HG), 1) // RWKV_HEAD
    same_head = rb == lb
    head_mask = same_head.astype(F32).astype(BF16)
    t_idx = lax.broadcasted_iota(jnp.int32, (C, HG), 0)
    s_idx = lax.broadcasted_iota(jnp.int32, (C, HG), 1) % C

    def bd(x):
        xb = x.astype(BF16)
        return jnp.concatenate([xb, xb, xb, xb], axis=0) * head_mask

    incl_c = [_order_masks(d, C)[1].astype(F32) for d in range(2)]
    G = {sq: _cumsum_rows(incl_c[sq[1]], ins[sq][3]) for sq in seqs}
    strict, incl, eye = {}, {}, {}
    for d in range(2):
        lead = (t_idx - s_idx) * (1 - 2 * d)
        strict[d], incl[d], eye[d] = lead > 0, lead >= 0, (lead == 0).astype(F32)
    q_t, r_t, p_t, k_t, p_hat, k_hat, v_g, e_end = {}, {}, {}, {}, {}, {}, {}, {}
    for sq in seqs:
        r, v, kk, lw, k, a = ins[sq]
        g_end = jnp.sum(lw, axis=0, keepdims=True)
        e_neg = jnp.exp(-G[sq])
        end = jnp.exp(g_end)
        full = dict(q=kk * jnp.exp(G[sq] - lw), r=r * jnp.exp(G[sq]), p=-(a * kk) * e_neg, k=k * e_neg, v=v)
        for g in range(n_groups):
            sl = slice(g * HG, (g + 1) * HG)
            c = sq + (g,)
            q_t[c], r_t[c], p_t[c], k_t[c], v_g[c] = (full[n][:, sl] for n in "qrpkv")
            e_end[c] = end[:, sl]
            p_hat[c], k_hat[c] = p_t[c] * e_end[c], k_t[c] * e_end[c]

    bv = {c: bd(v_g[c]) for c in chains}
    qr = {c: jnp.concatenate([q_t[c], r_t[c]], axis=0).astype(BF16) for c in chains}
    a_p = {c: _dot_nt(qr[c], bd(p_t[c])) for c in chains}
    a_k = {c: _dot_nt(qr[c], bd(k_t[c])) for c in chains}
    n_qp = {c: jnp.where(strict[c[1]], a_p[c][:C], 0.0) for c in chains}
    a_qk = {c: jnp.where(strict[c[1]], a_k[c][:C], 0.0) for c in chains}
    a_rp = {c: jnp.where(incl[c[1]], a_p[c][C:], 0.0) for c in chains}
    a_rk = {c: jnp.where(incl[c[1]], a_k[c][C:], 0.0) for c in chains}

    R = {c: _dot(n_qp[c], bd(n_qp[c])) for c in chains}
    T = {c: eye[c[1]] + n_qp[c] for c in chains}
    av = {c: _dot(a_qk[c], bv[c]) for c in chains}
    span = 4
    while span < C:
        RT = {c: _dot(jnp.concatenate([R[c], T[c]], axis=0), bd(R[c])) for c in chains}
        R = {c: RT[c][:C] for c in chains}
        T = {c: T[c] + RT[c][C:] for c in chains}
        span *= 2
    TR = {c: _dot(T[c], bd(R[c])) for c in chains}
    Tb = {c: (T[c] + TR[c]).astype(BF16) for c in chains}

    Sb = {c: S[c].astype(BF16) for c in chains}
    w = {c: _dot(Tb[c], bd(q_t[c])) for c in chains}
    u0 = {c: _dot(Tb[c], bd(av[c])) for c in chains}
    u = {c: _dot_nt(w[c], Sb[c]) + u0[c] for c in chains}
    o = {c: _dot(a_rp[c], bd(u[c])) + _dot(a_rk[c], bv[c]) + _dot_nt(r_t[c], Sb[c]) for c in chains}
    upd = {c: _dot_tn(jnp.concatenate([u[c], v_g[c]], axis=0),
                      jnp.concatenate([p_hat[c], k_hat[c]], axis=0)) for c in chains}
    S_new = {c: S[c] * e_end[c] + jnp.where(same_head, upd[c], 0.0) for c in chains}
    outs = {sq: jnp.concatenate([o[sq + (g,)] for g in range(n_groups)], axis=1) for sq in seqs}
    return outs, S_new


def _rwkv_call(r, v, kk, lw, k, a, n_ctx):
    B, N, W = r.shape
    C = RWKV_CHUNK
    assert C == RWKV_HEAD and W % (4 * RWKV_HEAD) == 0
    shared, per_dir = _scan_specs(C, W, n_ctx, N, B)
    specs = [s for d in range(2) for s in (shared(d), shared(d), shared(d), per_dir(d), per_dir(d), per_dir(d))]
    args = (r, v, kk, lw, k, a) * 2
    n_groups = W // (4 * RWKV_HEAD)
    return _scan_call(_rwkv_kernel, "rwkv7_scan", C, specs, args,
                      (2, n_groups, 4 * RWKV_HEAD, 4 * RWKV_HEAD), n_ctx)


def _ret_kernel(qf_ref, kf_ref, vf_ref, qb_ref, kb_ref, vb_ref, lg_ref, of_ref, ob_ref, s_ref):
    C = RET_CHUNK
    Dh = RET_HEAD

    @pl.when(pl.program_id(0) == 0)
    def _():
        s_ref[...] = jnp.zeros_like(s_ref)

    t = lax.broadcasted_iota(jnp.int32, (C, C), 0)
    s = lax.broadcasted_iota(jnp.int32, (C, C), 1)
    dist = jnp.abs(t - s).astype(F32)
    tcol = lax.broadcasted_iota(jnp.int32, (C, 1), 0)
    io = ((qf_ref, kf_ref, vf_ref, of_ref), (qb_ref, kb_ref, vb_ref, ob_ref))
    n_batch = s_ref.shape[0]
    n_heads = BRANCH_W // Dh
    chains = [(b, d, h) for b in range(n_batch) for d in range(2) for h in range(n_heads)]
    incl = [_order_masks(d, C)[1] for d in range(2)]
    n_t = [(tcol + 1 if d == 0 else C - tcol).astype(F32) for d in range(2)]

    def head(ref, b, h):
        return ref[b, :, h * Dh:(h + 1) * Dh].astype(F32)

    scores = {(b, d, h): _dot_nt(head(io[d][0], b, h), head(io[d][1], b, h)) for b, d, h in chains}
    outs, H = {}, {}
    for b, d, h in chains:
        lg = lg_ref[d, :, h * Dh:h * Dh + C]
        lgd = lg_ref[d, :, h * Dh:(h + 1) * Dh]
        A = scores[b, d, h] * jnp.where(incl[d], jnp.exp(lg * dist), 0.0)
        H[b, d, h] = s_ref[b, d, h]
        outs[b, d, h] = (_dot(A, head(io[d][2], b, h))
                         + _dot(head(io[d][0], b, h) * jnp.exp(lgd * n_t[d]), H[b, d, h]))
    for b, d, h in chains:
        lgd = lg_ref[d, :, h * Dh:(h + 1) * Dh]
        k_hat = head(io[d][1], b, h) * jnp.exp(lgd * (C - n_t[d]))
        s_ref[b, d, h] = H[b, d, h] * jnp.exp(lgd * C) + _dot_tn(k_hat, head(io[d][2], b, h))
    for b in range(n_batch):
        for d in range(2):
            io[d][3][b] = jnp.concatenate([outs[b, d, h] for h in range(n_heads)], axis=1).astype(BF16)


def _ret_call(q, k, v, v_col, lg, n_ctx):
    B, N, W = q.shape
    C = RET_CHUNK
    assert C <= RET_HEAD
    shared, _ = _scan_specs(C, W, n_ctx, N, B)
    specs = [s for d in range(2) for s in (shared(d), shared(d), shared(d, v_col))]
    specs.append(pl.BlockSpec((2, 1, W), lambda j: (0, 0, 0)))
    return _scan_call(_ret_kernel, "retention_scan", C, specs, (q, k, v, q, k, v, lg),
                      (2, W // RET_HEAD, RET_HEAD, RET_HEAD), n_ctx)


def _hgrn_kernel(qf_ref, vf_ref, kf_ref, lff_ref, qb_ref, vb_ref, kb_ref, lfb_ref, of_ref, ob_ref, s_ref):
    C = HGRN_CHUNK
    n_batch, _, n_heads = s_ref.shape[:3]

    @pl.when(pl.program_id(0) == 0)
    def _():
        s_ref[...] = jnp.zeros_like(s_ref)

    io = ((qf_ref, vf_ref, kf_ref, lff_ref, of_ref), (qb_ref, vb_ref, kb_ref, lfb_ref, ob_ref))
    n_chunks = HGRN_STEP // C
    order = [list(range(n_chunks)), list(reversed(range(n_chunks)))]
    seqs = [(b, d) for b in range(n_batch) for d in range(2)]
    blocks = [sq + (c,) for sq in seqs for c in range(n_chunks)]
    incl_c = [_order_masks(d, C)[1].astype(F32) for d in range(2)]

    def rows(ref, b, c):
        x = ref[b, c * C:(c + 1) * C, :] if len(ref.shape) == 3 else ref[0, b, c * C:(c + 1) * C, :]
        return x.astype(F32)

    G = {(b, d, c): _cumsum_rows(incl_c[d], rows(io[d][3], b, c)) for b, d, c in blocks}
    pre = {}
    for b, d, c in blocks:
        q_ref, v_ref, k_ref, lf_ref, _ = io[d]
        pre[b, d, c] = _hgrn_intra(d, rows(q_ref, b, c), rows(k_ref, b, c), rows(v_ref, b, c),
                                   rows(lf_ref, b, c), G[b, d, c])
    heads = range(n_heads)
    intra = {blk + (h,): _dot(pre[blk]["A"][h], pre[blk]["v"][h]) for blk in blocks for h in heads}

    S = {sq + (h,): s_ref[sq + (h,)] for sq in seqs for h in heads}
    for i in range(n_chunks):
        inter, S_next = {}, {}
        for b, d in seqs:
            blk = pre[b, d, order[d][i]]
            for h in heads:
                inter[b, d, h] = _dot_nt(blk["q_full"][h], S[b, d, h])
                S_next[b, d, h] = S[b, d, h] * blk["e_end"][h] + _dot_tn(blk["v"][h], blk["k_hat"][h])
        for b, d in seqs:
            c = order[d][i]
            io[d][4][b, c * C:(c + 1) * C, :] = jnp.concatenate(
                [intra[b, d, c, h] + inter[b, d, h] for h in heads], axis=1).astype(BF16)
        S = S_next
    for key, s_new in S.items():
        s_ref[key] = s_new


def _hgrn_intra(d, q, k, v, lf, G):
    C, SUB = HGRN_CHUNK, HGRN_SUB
    Dh = HGRN_HEAD
    n_sub = C // SUB
    _, incl = _order_masks(d, C)
    g_end = jnp.sum(lf, axis=0, keepdims=True)
    W = q.shape[-1]
    anchors = [G[I * SUB + SUB // 2:I * SUB + SUB // 2 + 1, :] for I in range(n_sub)]
    g_anchor = jnp.concatenate([jnp.broadcast_to(a, (SUB, W)) for a in anchors], axis=0)
    q_a = q * jnp.exp(G - g_anchor)
    q_full = q * jnp.exp(G)
    k_hat = k * jnp.exp(g_end - G)
    sub_of_row = lax.broadcasted_iota(jnp.int32, (C, 1), 0) // SUB
    k_anch = []
    for I in range(n_sub):
        visible = (I - sub_of_row) * (1 - 2 * d) >= 0
        k_anch.append(k * jnp.exp(jnp.where(visible, anchors[I] - G, 0.0)))
    e_end = jnp.exp(g_end)
    out = dict(A=[], v=[], q_full=[], k_hat=[], e_end=[])
    for h in range(W // Dh):
        sl = slice(h * Dh, (h + 1) * Dh)
        A = jnp.concatenate([_dot_nt(q_a[I * SUB:(I + 1) * SUB, sl], k_anch[I][:, sl]) for I in range(n_sub)],
                            axis=0)
        out["A"].append(jnp.where(incl, A, 0.0))
        out["v"].append(v[:, sl])
        out["q_full"].append(q_full[:, sl])
        out["k_hat"].append(k_hat[:, sl])
        out["e_end"].append(e_end[:, sl])
    return out


def _hgrn_call(q, v, v_col, k, lf, n_ctx):
    B, N, W = q.shape
    T = HGRN_STEP
    shared, per_dir = _scan_specs(T, W, n_ctx, N, B)
    specs = [s for d in range(2) for s in (shared(d), shared(d, v_col), per_dir(d), per_dir(d))]
    return _scan_call(_hgrn_kernel, "hgrn2_scan", T, specs, (q, v, k, lf) * 2,
                      (2, W // HGRN_HEAD, HGRN_HEAD, HGRN_HEAD), n_ctx)


def _rope_tables(n, head):
    half = head // 2
    inv_freq = ROPE_BASE ** (-jnp.arange(half, dtype=F32) / half)
    ang = jnp.arange(n).astype(F32)[:, None] * inv_freq[None, :]
    cos, sin = jnp.cos(ang), jnp.sin(ang)
    return jnp.concatenate([cos, cos], axis=1), jnp.concatenate([-sin, sin], axis=1)


def _block_diag_ones(width, head):
    idx = jnp.arange(width) // head
    return (idx[:, None] == idx[None, :]).astype(BF16)


def _permute_w_in(w_in_l, v_down_l):
    D = w_in_l.shape[0]
    W = BRANCH_W
    n_rw, n_ret, n_hg = 3 * W + LORA_W, 4 * W, 5 * W
    rw = w_in_l[:, :n_rw]
    ret = w_in_l[:, n_rw:n_rw + n_ret]
    hg = w_in_l[:, n_rw + n_ret:n_rw + n_ret + n_hg]
    gate = w_in_l[:, n_rw + n_ret + n_hg:]
    extra = jnp.zeros((D, VDOWN_PAD), F32)
    if v_down_l is not None:
        extra = extra.at[:, :v_down_l.shape[1]].set(v_down_l)
    main = jnp.concatenate([gate, rw[:, :3 * W], ret, hg[:, :W], hg[:, 3 * W:]], axis=1)
    decay = jnp.concatenate([rw[:, 3 * W:], extra, hg[:, W:3 * W]], axis=1)
    assert main.shape[1] == MAIN_WIDTH and decay.shape[1] == DECAY_WIDTH
    return main.astype(BF16), decay.astype(BF16)


def kernel(x, c, ctx, c_ctx, ada_w, ada_b, norm1_w, norm2_w, w_in, rwkv_mu, rwkv_w0, rwkv_w_up, rwkv_a0, rwkv_a_up, rwkv_g_up, rwkv_k_k, rwkv_k_a, rwkv_r_k, rwkv_lnx_w, rwkv_lnx_b, rwkv_v0, rwkv_v_down, rwkv_v_up, ret_decay, hgrn_lb, hgrn_norm_w, w_branch, w_out, mlp_w1, mlp_w2, final_norm_w):
    B, SEQ, D = x.shape
    n_ctx = ctx.shape[1]
    N = n_ctx + SEQ
    L = ada_w.shape[0]
    W = BRANCH_W
    assert n_ctx == ROW_TILE and SEQ % ROW_TILE == 0 and D == D_MODEL
    assert MAIN_COLS["rwkv"] % (3 * W) == 0 and MAIN_COLS["ret"] % W == 0

    xa = jnp.concatenate([ctx, x], axis=1)
    cvec = jnp.concatenate([c, c_ctx[None], jnp.zeros((8 - B - 1, D), F32)], axis=0)
    mod = _ada_call(cvec, ada_w, ada_b)
    sm = jax.nn.softmax(hgrn_lb.astype(F32), axis=0)
    hgrn_lower = jnp.cumsum(sm, axis=0) - sm[0:1]
    tables = _rope_tables(N, RET_HEAD)
    ones64 = _block_diag_ones(W, RWKV_HEAD)
    ones128 = _block_diag_ones(W, RET_HEAD)
    row = lambda t: t.reshape(1, -1)
    v_first = None
    out = None

    for l in range(L):
        last = l == L - 1
        mod_c = jnp.broadcast_to(mod[l, B].reshape(1, 6, D), (B, 6, D))
        modsel = jnp.concatenate([mod_c, mod[l, :B].reshape(B, 6, D)], axis=1)
        w_main, w_decay = _permute_w_in(w_in[l], None if l == 0 else rwkv_v_down[l - 1])
        P = _proj_call(xa, modsel, norm1_w[l], w_main, n_ctx, BF16, 4)
        Pd = _proj_call(xa, modsel, norm1_w[l], w_decay, n_ctx, F32, 1)

        has_vres = l > 0
        if has_vres:
            v0 = row(rwkv_v0[l - 1])
            v_up = jnp.zeros((VDOWN_PAD, W), F32).at[:rwkv_v_up.shape[1]].set(rwkv_v_up[l - 1]).astype(BF16)
        else:
            v0 = jnp.zeros((1, W), F32)
            v_up = jnp.zeros((VDOWN_PAD, W), BF16)
        prep_w = (row(rwkv_mu[l, :3 * W]), row(rwkv_mu[l, 3 * W:]),
                  rwkv_w_up[l].astype(BF16), rwkv_w0[l], rwkv_a_up[l].astype(BF16), rwkv_a0[l],
                  rwkv_g_up[l].astype(BF16), row(rwkv_k_k[l]), row(rwkv_k_a[l]), row(rwkv_r_k[l]),
                  v0, v_up, ones64, hgrn_lower[l])
        (r, v, kk, lw, kd, a, g, bonus, rq, rk, hq, hk, hlf) = _prep_call(
            P, Pd, P if v_first is None else v_first, tables, prep_w, n_ctx, has_vres)
        if v_first is None:
            v_first = v

        o_rw_dirs = _rwkv_call(r, v, kk, lw, kd, a, n_ctx)
        log_gamma = -jnp.exp(ret_decay[l].astype(F32))
        lg = jnp.repeat(log_gamma, RET_HEAD, axis=-1).reshape(2, 1, W)
        o_ret_dirs = _ret_call(rq, rk, P, (MAIN_COLS["ret"] + 2 * W) // W, lg, n_ctx)
        o_hg_dirs = _hgrn_call(hq, P, MAIN_COLS["hgrn_i"] // W, hk, hlf, n_ctx)

        merge_w = (row(rwkv_lnx_w[l]), row(rwkv_lnx_b[l]), row(hgrn_norm_w[l]), ones64, ones128,
                   w_branch[l].astype(BF16), w_out[l].astype(BF16))
        xm = _merge_call(xa, P, o_rw_dirs, g, bonus, o_ret_dirs, o_hg_dirs, merge_w, modsel, n_ctx, last)
        xa = _mlp_call(xm, modsel, norm2_w[l], mlp_w1[l].astype(BF16), mlp_w2[l].astype(BF16),
                       final_norm_w, n_ctx, (n_ctx // ROW_TILE) if last else 0, last)
        out = xa
    return out
```

```python
import functools

import jax
import jax.numpy as jnp
import numpy as np
from jax import lax
from jax.experimental import pallas as pl
from jax.experimental.pallas import tpu as pltpu

F32 = jnp.float32
BF16 = jnp.bfloat16

NORM_EPS = 1e-6
RWKV_LN_EPS = 64e-5
DECAY_SCALE = 0.6065306597126334
L2_EPS = 1e-12
ROPE_BASE = 10000.0
GRID_W = 64

BRANCH_W = 512
RWKV_HEAD = 64
RET_HEAD = 128
HGRN_HEAD = 128
DECAY_LORA = 64
AAA_LORA = 64
GATE_LORA = 128
LORA_W = 2 * DECAY_LORA + 2 * AAA_LORA + GATE_LORA
VDOWN_PAD = 128

RWKV_CHUNK = 64
HGRN_CHUNK = 64
HGRN_SUB = 32
HGRN_STEP = 128
RET_CHUNK = 128
ROW_TILE = 256

VMEM_LIMIT = 56 * 1024 * 1024

D_MODEL = 2 * BRANCH_W
MAIN_COLS = dict(gate=0, rwkv=3 * D_MODEL, ret=3 * D_MODEL + 3 * BRANCH_W, hgrn_q=3 * D_MODEL + 7 * BRANCH_W,
                 hgrn_i=3 * D_MODEL + 8 * BRANCH_W, hgrn_g=3 * D_MODEL + 9 * BRANCH_W)
MAIN_WIDTH = 3 * D_MODEL + 10 * BRANCH_W
DECAY_COLS = dict(lora=0, v_down=LORA_W, hgrn_f=LORA_W + VDOWN_PAD)
DECAY_WIDTH = LORA_W + VDOWN_PAD + 2 * BRANCH_W


def _sigmoid(x):
    return 1.0 / (1.0 + jnp.exp(-x))


def _dot(a, b):
    return jnp.dot(a.astype(BF16), b.astype(BF16), preferred_element_type=F32)


def _dot_nt(a, b):
    return lax.dot_general(a.astype(BF16), b.astype(BF16), (((1,), (1,)), ((), ())),
                           preferred_element_type=F32)


def _dot_tn(a, b):
    return lax.dot_general(a.astype(BF16), b.astype(BF16), (((0,), (0,)), ((), ())),
                           preferred_element_type=F32)


def _cumsum_rows(mask_f32, x):
    m = mask_f32.astype(BF16)
    hi = x.astype(BF16)
    lo = (x - hi.astype(F32)).astype(BF16)
    return jnp.dot(m, hi, preferred_element_type=F32) + jnp.dot(m, lo, preferred_element_type=F32)


def _head_sum(x, ones_bd, split=True):
    hi = x.astype(BF16)
    out = jnp.dot(hi, ones_bd, preferred_element_type=F32)
    if split:
        lo = (x - hi.astype(F32)).astype(BF16)
        out = out + jnp.dot(lo, ones_bd, preferred_element_type=F32)
    return out


def _ada_kernel(c_ref, w_ref, b_ref, o_ref):
    cv = c_ref[...]
    s = cv * _sigmoid(cv)
    o_ref[0] = _dot(s, w_ref[0]) + b_ref[0]


def _ada_call(cvec, ada_w, ada_b):
    L, D, D6 = ada_w.shape
    tn = D6 // 4
    return pl.pallas_call(
        _ada_kernel,
        grid=(L, D6 // tn),
        in_specs=[pl.BlockSpec((8, D), lambda l, j: (0, 0)),
                  pl.BlockSpec((1, D, tn), lambda l, j: (l, 0, j)),
                  pl.BlockSpec((1, 1, tn), lambda l, j: (l, 0, j))],
        out_specs=pl.BlockSpec((1, 8, tn), lambda l, j: (l, 0, j)),
        out_shape=jax.ShapeDtypeStruct((L, 8, D6), F32),
        compiler_params=pltpu.CompilerParams(vmem_limit_bytes=VMEM_LIMIT),
        name="ada_mod",
    )(cvec, ada_w, ada_b.reshape(L, 1, D6))


def _modulated_norm(xf, nw, mod_ref, is_ctx, which):
    y = xf * lax.rsqrt(jnp.mean(xf * xf, axis=-1, keepdims=True) + NORM_EPS) * nw
    sh = jnp.where(is_ctx, mod_ref[0, which:which + 1, :], mod_ref[0, 6 + which:7 + which, :])
    sc = jnp.where(is_ctx, mod_ref[0, which + 1:which + 2, :], mod_ref[0, 7 + which:8 + which, :])
    return y * (1.0 + sc) + sh


def _is_ctx_rows(tile_idx, tm, n_ctx):
    row = tile_idx * tm + lax.broadcasted_iota(jnp.int32, (tm, 1), 0)
    return row < n_ctx


def _proj_kernel(x_ref, mod_ref, nw_ref, w_ref, o_ref, h_ref, *, tm, n_ctx):
    i = pl.program_id(1)

    @pl.when(pl.program_id(2) == 0)
    def _():
        h = _modulated_norm(x_ref[0], nw_ref[...], mod_ref, _is_ctx_rows(i, tm, n_ctx), 0)
        h_ref[...] = h.astype(BF16)

    o_ref[0] = jnp.dot(h_ref[...], w_ref[...], preferred_element_type=F32).astype(o_ref.dtype)


def _proj_call(xa, modsel, nw, w, n_ctx, out_dtype, n_tiles, tm=768):
    B, N, D = xa.shape
    NP = w.shape[1]
    tn = NP // n_tiles
    return pl.pallas_call(
        functools.partial(_proj_kernel, tm=tm, n_ctx=n_ctx),
        grid=(B, N // tm, n_tiles),
        in_specs=[pl.BlockSpec((1, tm, D), lambda b, i, j: (b, i, 0)),
                  pl.BlockSpec((1, 12, D), lambda b, i, j: (b, 0, 0)),
                  pl.BlockSpec((1, D), lambda b, i, j: (0, 0)),
                  pl.BlockSpec((D, tn), lambda b, i, j: (0, j))],
        out_specs=pl.BlockSpec((1, tm, tn), lambda b, i, j: (b, i, j)),
        out_shape=jax.ShapeDtypeStruct((B, N, NP), out_dtype),
        scratch_shapes=[pltpu.VMEM((tm, D), BF16)],
        compiler_params=pltpu.CompilerParams(
            dimension_semantics=("parallel", "parallel", "arbitrary"),
            vmem_limit_bytes=VMEM_LIMIT),
        name="in_proj",
    )(xa, modsel, nw.reshape(1, D), w)


def _shift_matrices(tm, seq):
    mats = np.zeros((5, tm, tm + 2 * GRID_W), np.float32)
    t = np.arange(tm)
    centre = t + GRID_W
    mats[0, t[1:], centre[1:] - 1] = 0.5
    mats[0, t[:-1], centre[:-1] + 1] = 0.5
    for kind, (first, last) in enumerate(((True, False), (False, False), (False, True), (True, True)), start=1):
        has_left = t % GRID_W != 0
        has_right = t % GRID_W != GRID_W - 1
        has_up = ~(first & (t < GRID_W))
        has_down = ~(last & (t >= tm - GRID_W))
        mats[kind, t[has_left], centre[has_left] - 1] = 0.25
        mats[kind, t[has_right], centre[has_right] + 1] = 0.25
        mats[kind, t[has_up], centre[has_up] - GRID_W] = 0.25
        mats[kind, t[has_down], centre[has_down] + GRID_W] = 0.25
    n_lat = seq // tm

    def kind_of_tile(i):
        latent = jnp.where(i == 1, 4 if n_lat == 1 else 1, jnp.where(i == n_lat, 3, 2))
        return jnp.where(i == 0, 0, latent)

    return jnp.asarray(mats, BF16), kind_of_tile


def _shift_mix(cur, prev, nxt, mu, s):
    ext = jnp.concatenate([prev, cur, nxt], axis=0)
    if ext.dtype == BF16:
        shifted = jnp.dot(s, ext, preferred_element_type=F32)
    else:
        hi = ext.astype(BF16)
        lo = (ext - hi.astype(F32)).astype(BF16)
        shifted = jnp.dot(s, hi, preferred_element_type=F32) + jnp.dot(s, lo, preferred_element_type=F32)
    cur = cur.astype(F32)
    return cur + mu * (shifted - cur)


def _prep_kernel(pc_ref, pp_ref, pn_ref, lc_ref, lp_ref, ln_ref, hv_ref, vf_ref, rq_ref, rk_ref, cs_ref, sn_ref,
                 hq_ref, hf0_ref, hf1_ref,
                 mum_ref, mul_ref, wup_ref, w0_ref, aup_ref, a0_ref, gup_ref, kk_w_ref, ka_ref, r_k_ref,
                 v0_ref, vup_ref, ones_ref, lb_ref,
                 shift_ref,
                 r_o, v_o, kk_o, lw_o, kd_o, a_o, g_o, bonus_o, rq_o, rk_o, hq_o, hk_o, hlf_o,
                 *, has_vres):
    W = BRANCH_W

    p = _shift_mix(pc_ref[0], pp_ref[0], pn_ref[0], mum_ref[...], shift_ref[0])
    lo = _shift_mix(lc_ref[0], lp_ref[0], ln_ref[0], mul_ref[...], shift_ref[0])
    r, k, v = p[:, :W], p[:, W:2 * W], p[:, 2 * W:]
    if has_vres:
        mix = _sigmoid(v0_ref[...] + _dot(hv_ref[0], vup_ref[...]))
        v = v + (vf_ref[0].astype(F32) - v) * mix
    r_o[0] = r.astype(r_o.dtype)
    v_o[0] = v.astype(v_o.dtype)
    gd = lo[:, 2 * DECAY_LORA + 2 * AAA_LORA:]
    g_o[0] = _dot(_sigmoid(gd), gup_ref[...]).astype(g_o.dtype)
    kx = k * kk_w_ref[...]
    kk_o[0] = (kx * lax.rsqrt(_head_sum(kx * kx, ones_ref[...], split=False) + L2_EPS)).astype(kk_o.dtype)
    k_sum = None
    for d in range(2):
        wd = lo[:, d * DECAY_LORA:(d + 1) * DECAY_LORA]
        ad = lo[:, 2 * DECAY_LORA + d * AAA_LORA:2 * DECAY_LORA + (d + 1) * AAA_LORA]
        z = -(w0_ref[d:d + 1, :] + _dot(jnp.tanh(wd), wup_ref[d]))
        lw_o[d, 0] = -DECAY_SCALE / (1.0 + jnp.exp(z))
        a = _sigmoid(a0_ref[d:d + 1, :] + _dot(ad, aup_ref[d]))
        a_o[d, 0] = a.astype(a_o.dtype)
        kd = k * (1.0 + (a - 1.0) * ka_ref[...])
        kd_o[d, 0] = kd.astype(kd_o.dtype)
        k_sum = kd if k_sum is None else k_sum + kd
    bonus_o[0] = (_head_sum(r * k_sum * r_k_ref[...], ones_ref[...], split=False) * v).astype(bonus_o.dtype)

    cosf = jnp.concatenate([cs_ref[...]] * (W // RET_HEAD), axis=1)
    sinf = jnp.concatenate([sn_ref[...]] * (W // RET_HEAD), axis=1)
    lane = lax.broadcasted_iota(jnp.int32, (1, W), 1)
    first_half = (lane & (RET_HEAD - 1)) < RET_HEAD // 2

    def rope(t):
        rot = jnp.where(first_half, pltpu.roll(t, W - RET_HEAD // 2, 1), pltpu.roll(t, RET_HEAD // 2, 1))
        return t * cosf + rot * sinf

    rq_o[0] = rope(rq_ref[0].astype(F32)).astype(rq_o.dtype)
    rk_o[0] = (rope(rk_ref[0].astype(F32)) * (RET_HEAD ** -0.5)).astype(rk_o.dtype)

    q = hq_ref[0].astype(F32)
    hq_o[0] = (q * _sigmoid(q)).astype(hq_o.dtype)
    for d, f_ref in enumerate((hf0_ref, hf1_ref)):
        f = f_ref[0]
        e = jnp.exp(-jnp.abs(f))
        inv = 1.0 / (1.0 + e)
        sig_pos = jnp.where(f >= 0, inv, e * inv)
        sig_neg = jnp.where(f >= 0, e * inv, inv)
        lb = lb_ref[d:d + 1, :]
        hlf_o[d, 0] = jnp.log(lb + (1.0 - lb) * sig_pos)
        hk_o[d, 0] = ((1.0 - lb) * sig_neg).astype(hk_o.dtype)


def _prep_call(P, Pd, v_first, tables, wts, n_ctx, has_vres):
    B, N, _ = P.shape
    W = BRANCH_W
    tm = ROW_TILE
    seq = N - n_ctx
    halo_per_tile = tm // GRID_W
    n_halo = N // GRID_W
    c_main = MAIN_COLS["rwkv"] // (3 * W)
    c_rq = MAIN_COLS["ret"] // W
    c_hq = MAIN_COLS["hgrn_q"] // W
    c_lora = DECAY_COLS["lora"] // LORA_W
    c_vd = DECAY_COLS["v_down"] // VDOWN_PAD
    c_hf = DECAY_COLS["hgrn_f"] // W

    def cur(width, col):
        return pl.BlockSpec((1, tm, width), lambda b, i: (b, i, col))

    def prev(width, col):
        return pl.BlockSpec((1, GRID_W, width), lambda b, i: (b, jnp.maximum(i * halo_per_tile - 1, 0), col))

    def nxt(width, col):
        return pl.BlockSpec((1, GRID_W, width),
                            lambda b, i: (b, jnp.minimum((i + 1) * halo_per_tile, n_halo - 1), col))

    def const(shape):
        return pl.BlockSpec(shape, lambda b, i: (0,) * len(shape))

    tok = pl.BlockSpec((1, tm, W), lambda b, i: (b, i, 0))
    tok2 = pl.BlockSpec((2, 1, tm, W), lambda b, i: (0, b, i, 0))
    cos, sin = tables
    in_specs = [cur(3 * W, c_main), prev(3 * W, c_main), nxt(3 * W, c_main),
                cur(LORA_W, c_lora), prev(LORA_W, c_lora), nxt(LORA_W, c_lora),
                cur(VDOWN_PAD, c_vd), tok, cur(W, c_rq), cur(W, c_rq + 1),
                pl.BlockSpec((tm, RET_HEAD), lambda b, i: (i, 0)),
                pl.BlockSpec((tm, RET_HEAD), lambda b, i: (i, 0)),
                cur(W, c_hq), cur(W, c_hf), cur(W, c_hf + 1)]
    in_specs += [const(w.shape) for w in wts]
    shift_mats, kind_of_tile = _shift_matrices(tm, seq)
    in_specs.append(pl.BlockSpec((1,) + shift_mats.shape[1:], lambda b, i: (kind_of_tile(i), 0, 0)))
    one = jax.ShapeDtypeStruct((B, N, W), BF16)
    two = jax.ShapeDtypeStruct((2, B, N, W), BF16)
    two_f32 = jax.ShapeDtypeStruct((2, B, N, W), F32)
    out_shape = [one, one, one, two_f32, two, two, one, one, one, one, one, two, two_f32]
    out_specs = [tok, tok, tok, tok2, tok2, tok2, tok, tok, tok, tok, tok, tok2, tok2]
    return pl.pallas_call(
        functools.partial(_prep_kernel, has_vres=has_vres),
        grid=(B, N // tm),
        in_specs=in_specs,
        out_specs=out_specs,
        out_shape=out_shape,
        compiler_params=pltpu.CompilerParams(
            dimension_semantics=("parallel", "parallel"), vmem_limit_bytes=VMEM_LIMIT),
        name="branch_prep",
    )(P, P, P, Pd, Pd, Pd, Pd, v_first, P, P, cos, sin, P, Pd, Pd, *wts, shift_mats)


def _merge_kernel(x_ref, g0_ref, g1_ref, g2_ref, orwf_ref, orwb_ref, g_ref, bonus_ref, oretf_ref, oretb_ref,
                  rg_ref, ohgf_ref, ohgb_ref, hg_ref,
                  lnw_ref, lnb_ref, hnw_ref, m64_ref, m128_ref, wb_ref, wo_ref, mod_ref, o_ref,
                  *, tm, n_ctx, tile_off):
    i = pl.program_id(1) + tile_off

    o = orwf_ref[0].astype(F32) + orwb_ref[0].astype(F32)
    cen = o - _head_sum(o, m64_ref[...]) * (1.0 / RWKV_HEAD)
    var = _head_sum(cen * cen, m64_ref[...], split=False) * (1.0 / RWKV_HEAD)
    y_rw = ((cen * lax.rsqrt(var + RWKV_LN_EPS) * lnw_ref[...] + lnb_ref[...] + bonus_ref[0].astype(F32))
            * g_ref[0].astype(F32))
    o = oretf_ref[0].astype(F32) + oretb_ref[0].astype(F32)
    ms = _head_sum(o * o, m128_ref[...], split=False) * (1.0 / RET_HEAD)
    rg = rg_ref[0].astype(F32)
    y_ret = o * lax.rsqrt(ms + NORM_EPS) * (rg * _sigmoid(rg))
    o = ohgf_ref[0].astype(F32) + ohgb_ref[0].astype(F32)
    ms = _head_sum(o * o, m128_ref[...], split=False) * (1.0 / HGRN_HEAD)
    hg = hg_ref[0].astype(F32)
    y_hg = o * lax.rsqrt(ms + NORM_EPS) * hnw_ref[...] * (hg * _sigmoid(hg))

    merged = None
    for b, (y, gate_ref) in enumerate(((y_rw, g0_ref), (y_ret, g1_ref), (y_hg, g2_ref))):
        z = _sigmoid(gate_ref[0].astype(F32)) * _dot(y, wb_ref[b])
        merged = z if merged is None else merged + z
    out = _dot(merged, wo_ref[...])
    g1 = jnp.where(_is_ctx_rows(i, tm, n_ctx), mod_ref[0, 2:3, :], mod_ref[0, 8:9, :])
    o_ref[0] = x_ref[0] + g1 * out


def _merge_call(xa, P, o_rw, g, bonus, o_ret, o_hg, wts, modsel, n_ctx, latent_only):
    B, N, D = xa.shape
    W = BRANCH_W
    tm = ROW_TILE
    off = n_ctx // tm if latent_only else 0
    n_rows = N - off * tm
    c_gate = MAIN_COLS["gate"] // D
    c_rg = (MAIN_COLS["ret"] + 3 * W) // W
    c_hg = MAIN_COLS["hgrn_g"] // W

    def col(width, c):
        return pl.BlockSpec((1, tm, width), lambda b, i: (b, i + off, c))

    def const(shape):
        return pl.BlockSpec(shape, lambda b, i: (0,) * len(shape))

    tok = col(W, 0)
    return pl.pallas_call(
        functools.partial(_merge_kernel, tm=tm, n_ctx=n_ctx, tile_off=off),
        grid=(B, n_rows // tm),
        in_specs=[col(D, 0), col(D, c_gate), col(D, c_gate + 1), col(D, c_gate + 2),
                  tok, tok, tok, tok, tok, tok, col(W, c_rg), tok, tok, col(W, c_hg)]
                 + [const(w.shape) for w in wts]
                 + [pl.BlockSpec((1, 12, D), lambda b, i: (b, 0, 0))],
        out_specs=pl.BlockSpec((1, tm, D), lambda b, i: (b, i, 0)),
        out_shape=jax.ShapeDtypeStruct((B, n_rows, D), F32),
        compiler_params=pltpu.CompilerParams(
            dimension_semantics=("parallel", "parallel"), vmem_limit_bytes=VMEM_LIMIT),
        name="merge_out",
    )(xa, P, P, P, o_rw[0], o_rw[1], g, bonus, o_ret[0], o_ret[1], P, o_hg[0], o_hg[1], P, *wts, modsel)


def _mlp_kernel(x_ref, mod_ref, nw_ref, w1_ref, w2_ref, fw_ref, o_ref, *, tm, n_ctx, tile_off, final):
    i = pl.program_id(1) + tile_off
    is_ctx = _is_ctx_rows(i, tm, n_ctx)
    xf = x_ref[0]
    h = _modulated_norm(xf, nw_ref[...], mod_ref, is_ctx, 3)
    a = jnp.maximum(_dot(h, w1_ref[...]), 0.0)
    out = _dot(a * a, w2_ref[...])
    g2 = jnp.where(is_ctx, mod_ref[0, 5:6, :], mod_ref[0, 11:12, :])
    xn = xf + g2 * out
    if final:
        xn = xn * lax.rsqrt(jnp.mean(xn * xn, axis=-1, keepdims=True) + NORM_EPS) * fw_ref[...]
    o_ref[0] = xn


def _mlp_call(xa, modsel, nw, w1, w2, fw, n_ctx, tile_off, final):
    B, n_rows, D = xa.shape
    DF = w1.shape[1]
    tm = ROW_TILE
    return pl.pallas_call(
        functools.partial(_mlp_kernel, tm=tm, n_ctx=n_ctx, tile_off=tile_off, final=final),
        grid=(B, n_rows // tm),
        in_specs=[pl.BlockSpec((1, tm, D), lambda b, i: (b, i, 0)),
                  pl.BlockSpec((1, 12, D), lambda b, i: (b, 0, 0)),
                  pl.BlockSpec((1, D), lambda b, i: (0, 0)),
                  pl.BlockSpec((D, DF), lambda b, i: (0, 0)),
                  pl.BlockSpec((DF, D), lambda b, i: (0, 0)),
                  pl.BlockSpec((1, D), lambda b, i: (0, 0))],
        out_specs=pl.BlockSpec((1, tm, D), lambda b, i: (b, i, 0)),
        out_shape=jax.ShapeDtypeStruct((B, n_rows, D), F32),
        compiler_params=pltpu.CompilerParams(
            dimension_semantics=("parallel", "parallel"), vmem_limit_bytes=VMEM_LIMIT),
        name="mlp",
    )(xa, modsel, nw.reshape(1, D), w1, w2, fw.reshape(1, D))


def _order_masks(d, n):
    t = lax.broadcasted_iota(jnp.int32, (n, n), 0)
    s = lax.broadcasted_iota(jnp.int32, (n, n), 1)
    lead = (t - s) * (1 - 2 * d)
    return lead > 0, lead >= 0


def _scan_specs(T, W, n_ctx, N, B):
    n_ctx_blocks, n_blocks = n_ctx // T, N // T

    def block(d, j):
        if d == 0:
            return j
        return jnp.where(j < n_ctx_blocks, n_ctx_blocks - 1 - j, n_blocks - 1 + n_ctx_blocks - j)

    def shared(d, col=0):
        return pl.BlockSpec((B, T, W), lambda j: (0, block(d, j), col))

    def per_dir(d):
        return pl.BlockSpec((1, B, T, W), lambda j: (d, 0, block(d, j), 0))

    return shared, per_dir


def _scan_call(kernel_fn, name, T, specs, args, state_shape, n_ctx):
    B, N, W = args[0].shape[0], args[0].shape[1], BRANCH_W
    shared, _ = _scan_specs(T, W, n_ctx, N, B)
    out = jax.ShapeDtypeStruct((B, N, W), BF16)
    return pl.pallas_call(
        kernel_fn,
        grid=(N // T,),
        in_specs=specs,
        out_specs=[shared(0), shared(1)],
        out_shape=[out, out],
        scratch_shapes=[pltpu.VMEM((B,) + state_shape, F32)],
        compiler_params=pltpu.CompilerParams(
            dimension_semantics=("arbitrary",), vmem_limit_bytes=VMEM_LIMIT),
        name=name,
    )(*args)


def _load(ref, b):
    x = ref[b] if len(ref.shape) == 3 else ref[0, b]
    return x.astype(F32)


def _rwkv_kernel(*refs):
    sides, (of_ref, ob_ref, s_ref) = (refs[0:6], refs[6:12]), refs[12:]
    n_batch, _, n_groups = s_ref.shape[:3]

    @pl.when(pl.program_id(0) == 0)
    def _():
        s_ref[...] = jnp.zeros_like(s_ref)

    seqs = [(b, d) for b in range(n_batch) for d in range(2)]
    ins = {(b, d): tuple(_load(ref, b) for ref in sides[d]) for b, d in seqs}
    S = {(b, d, g): s_ref[b, d, g] for b, d in seqs for g in range(n_groups)}
    outs, S = _rwkv_step(seqs, ins, S, n_groups)
    for b, d in seqs:
        (of_ref, ob_ref)[d][b] = outs[b, d].astype(BF16)
    for key, s_new in S.items():
        s_ref[key] = s_new


def _rwkv_step(seqs, ins, S, n_groups):
    C = RWKV_CHUNK
    HG = 4 * RWKV_HEAD
    chains = [sq + (g,) for sq in seqs for g in range(n_groups)]

    rb = lax.broadcasted_iota(jnp.int32, (4 * C, HG), 0) // C
    lb = lax.broadcasted_iota(jnp.int32, (4 * C, HG), 1) // RWKV_HEAD
    same_head = rb == lb
    head_mask = same_head.astype(F32).astype(BF16)
    t_idx = lax.broadcasted_iota(jnp.int32, (C, HG), 0)
    s_idx = lax.broadcasted_iota(jnp.int32, (C, HG), 1) % C

    def bd(x):
        xb = x.astype(BF16)
        return jnp.concatenate([xb, xb, xb, xb], axis=0) * head_mask

    incl_c = [_order_masks(d, C)[1].astype(F32) for d in range(2)]
    G = {sq: _cumsum_rows(incl_c[sq[1]], ins[sq][3]) for sq in seqs}
    strict, incl, eye = {}, {}, {}
    for d in range(2):
        lead = (t_idx - s_idx) * (1 - 2 * d)
        strict[d], incl[d], eye[d] = lead > 0, lead >= 0, (lead == 0).astype(F32)
    q_t, r_t, p_t, k_t, p_hat, k_hat, v_g, e_end = {}, {}, {}, {}, {}, {}, {}, {}
    for sq in seqs:
        r, v, kk, lw, k, a = ins[sq]
        g_end = jnp.sum(lw, axis=0, keepdims=True)
        e_neg = jnp.exp(-G[sq])
        end = jnp.exp(g_end)
        full = dict(q=kk * jnp.exp(G[sq] - lw), r=r * jnp.exp(G[sq]), p=-(a * kk) * e_neg, k=k * e_neg, v=v)
        for g in range(n_groups):
            sl = slice(g * HG, (g + 1) * HG)
            c = sq + (g,)
            q_t[c], r_t[c], p_t[c], k_t[c], v_g[c] = (full[n][:, sl] for n in "qrpkv")
            e_end[c] = end[:, sl]
            p_hat[c], k_hat[c] = p_t[c] * e_end[c], k_t[c] * e_end[c]

    bv = {c: bd(v_g[c]) for c in chains}
    qr = {c: jnp.concatenate([q_t[c], r_t[c]], axis=0).astype(BF16) for c in chains}
    a_p = {c: _dot_nt(qr[c], bd(p_t[c])) for c in chains}
    a_k = {c: _dot_nt(qr[c], bd(k_t[c])) for c in chains}
    n_qp = {c: jnp.where(strict[c[1]], a_p[c][:C], 0.0) for c in chains}
    a_qk = {c: jnp.where(strict[c[1]], a_k[c][:C], 0.0) for c in chains}
    a_rp = {c: jnp.where(incl[c[1]], a_p[c][C:], 0.0) for c in chains}
    a_rk = {c: jnp.where(incl[c[1]], a_k[c][C:], 0.0) for c in chains}

    R = {c: _dot(n_qp[c], bd(n_qp[c])) for c in chains}
    T = {c: eye[c[1]] + n_qp[c] for c in chains}
    av = {c: _dot(a_qk[c], bv[c]) for c in chains}
    span = 4
    while span < C:
        RT = {c: _dot(jnp.concatenate([R[c], T[c]], axis=0), bd(R[c])) for c in chains}
        R = {c: RT[c][:C] for c in chains}
        T = {c: T[c] + RT[c][C:] for c in chains}
        span *= 2
    TR = {c: _dot(T[c], bd(R[c])) for c in chains}
    Tb = {c: (T[c] + TR[c]).astype(BF16) for c in chains}

    Sb = {c: S[c].astype(BF16) for c in chains}
    w = {c: _dot(Tb[c], bd(q_t[c])) for c in chains}
    u0 = {c: _dot(Tb[c], bd(av[c])) for c in chains}
    u = {c: _dot_nt(w[c], Sb[c]) + u0[c] for c in chains}
    o = {c: _dot(a_rp[c], bd(u[c])) + _dot(a_rk[c], bv[c]) + _dot_nt(r_t[c], Sb[c]) for c in chains}
    upd = {c: _dot_tn(jnp.concatenate([u[c], v_g[c]], axis=0),
                      jnp.concatenate([p_hat[c], k_hat[c]], axis=0)) for c in chains}
    S_new = {c: S[c] * e_end[c] + jnp.where(same_head, upd[c], 0.0) for c in chains}
    outs = {sq: jnp.concatenate([o[sq + (g,)] for g in range(n_groups)], axis=1) for sq in seqs}
    return outs, S_new


def _rwkv_call(r, v, kk, lw, k, a, n_ctx):
    B, N, W = r.shape
    C = RWKV_CHUNK
    assert C == RWKV_HEAD and W % (4 * RWKV_HEAD) == 0
    shared, per_dir = _scan_specs(C, W, n_ctx, N, B)
    specs = [s for d in range(2) for s in (shared(d), shared(d), shared(d), per_dir(d), per_dir(d), per_dir(d))]
    args = (r, v, kk, lw, k, a) * 2
    n_groups = W // (4 * RWKV_HEAD)
    return _scan_call(_rwkv_kernel, "rwkv7_scan", C, specs, args,
                      (2, n_groups, 4 * RWKV_HEAD, 4 * RWKV_HEAD), n_ctx)


def _ret_kernel(qf_ref, kf_ref, vf_ref, qb_ref, kb_ref, vb_ref, lg_ref, of_ref, ob_ref, s_ref):
    C = RET_CHUNK
    Dh = RET_HEAD

    @pl.when(pl.program_id(0) == 0)
    def _():
        s_ref[...] = jnp.zeros_like(s_ref)

    t = lax.broadcasted_iota(jnp.int32, (C, C), 0)
    s = lax.broadcasted_iota(jnp.int32, (C, C), 1)
    dist = jnp.abs(t - s).astype(F32)
    tcol = lax.broadcasted_iota(jnp.int32, (C, 1), 0)
    io = ((qf_ref, kf_ref, vf_ref, of_ref), (qb_ref, kb_ref, vb_ref, ob_ref))
    n_batch = s_ref.shape[0]
    n_heads = BRANCH_W // Dh
    chains = [(b, d, h) for b in range(n_batch) for d in range(2) for h in range(n_heads)]
    incl = [_order_masks(d, C)[1] for d in range(2)]
    n_t = [(tcol + 1 if d == 0 else C - tcol).astype(F32) for d in range(2)]

    def head(ref, b, h):
        return ref[b, :, h * Dh:(h + 1) * Dh].astype(F32)

    scores = {(b, d, h): _dot_nt(head(io[d][0], b, h), head(io[d][1], b, h)) for b, d, h in chains}
    outs, H = {}, {}
    for b, d, h in chains:
        lg = lg_ref[d, :, h * Dh:h * Dh + C]
        lgd = lg_ref[d, :, h * Dh:(h + 1) * Dh]
        A = scores[b, d, h] * jnp.where(incl[d], jnp.exp(lg * dist), 0.0)
        H[b, d, h] = s_ref[b, d, h]
        outs[b, d, h] = (_dot(A, head(io[d][2], b, h))
                         + _dot(head(io[d][0], b, h) * jnp.exp(lgd * n_t[d]), H[b, d, h]))
    for b, d, h in chains:
        lgd = lg_ref[d, :, h * Dh:(h + 1) * Dh]
        k_hat = head(io[d][1], b, h) * jnp.exp(lgd * (C - n_t[d]))
        s_ref[b, d, h] = H[b, d, h] * jnp.exp(lgd * C) + _dot_tn(k_hat, head(io[d][2], b, h))
    for b in range(n_batch):
        for d in range(2):
            io[d][3][b] = jnp.concatenate([outs[b, d, h] for h in range(n_heads)], axis=1).astype(BF16)


def _ret_call(q, k, v, v_col, lg, n_ctx):
    B, N, W = q.shape
    C = RET_CHUNK
    assert C <= RET_HEAD
    shared, _ = _scan_specs(C, W, n_ctx, N, B)
    specs = [s for d in range(2) for s in (shared(d), shared(d), shared(d, v_col))]
    specs.append(pl.BlockSpec((2, 1, W), lambda j: (0, 0, 0)))
    return _scan_call(_ret_kernel, "retention_scan", C, specs, (q, k, v, q, k, v, lg),
                      (2, W // RET_HEAD, RET_HEAD, RET_HEAD), n_ctx)


def _hgrn_kernel(qf_ref, vf_ref, kf_ref, lff_ref, qb_ref, vb_ref, kb_ref, lfb_ref, of_ref, ob_ref, s_ref):
    C = HGRN_CHUNK
    n_batch, _, n_heads = s_ref.shape[:3]

    @pl.when(pl.program_id(0) == 0)
    def _():
        s_ref[...] = jnp.zeros_like(s_ref)

    io = ((qf_ref, vf_ref, kf_ref, lff_ref, of_ref), (qb_ref, vb_ref, kb_ref, lfb_ref, ob_ref))
    n_chunks = HGRN_STEP // C
    order = [list(range(n_chunks)), list(reversed(range(n_chunks)))]
    seqs = [(b, d) for b in range(n_batch) for d in range(2)]
    blocks = [sq + (c,) for sq in seqs for c in range(n_chunks)]
    incl_c = [_order_masks(d, C)[1].astype(F32) for d in range(2)]

    def rows(ref, b, c):
        x = ref[b, c * C:(c + 1) * C, :] if len(ref.shape) == 3 else ref[0, b, c * C:(c + 1) * C, :]
        return x.astype(F32)

    G = {(b, d, c): _cumsum_rows(incl_c[d], rows(io[d][3], b, c)) for b, d, c in blocks}
    pre = {}
    for b, d, c in blocks:
        q_ref, v_ref, k_ref, lf_ref, _ = io[d]
        pre[b, d, c] = _hgrn_intra(d, rows(q_ref, b, c), rows(k_ref, b, c), rows(v_ref, b, c),
                                   rows(lf_ref, b, c), G[b, d, c])
    heads = range(n_heads)
    intra = {blk + (h,): _dot(pre[blk]["A"][h], pre[blk]["v"][h]) for blk in blocks for h in heads}

    S = {sq + (h,): s_ref[sq + (h,)] for sq in seqs for h in heads}
    for i in range(n_chunks):
        inter, S_next = {}, {}
        for b, d in seqs:
            blk = pre[b, d, order[d][i]]
            for h in heads:
                inter[b, d, h] = _dot_nt(blk["q_full"][h], S[b, d, h])
                S_next[b, d, h] = S[b, d, h] * blk["e_end"][h] + _dot_tn(blk["v"][h], blk["k_hat"][h])
        for b, d in seqs:
            c = order[d][i]
            io[d][4][b, c * C:(c + 1) * C, :] = jnp.concatenate(
                [intra[b, d, c, h] + inter[b, d, h] for h in heads], axis=1).astype(BF16)
        S = S_next
    for key, s_new in S.items():
        s_ref[key] = s_new


def _hgrn_intra(d, q, k, v, lf, G):
    C, SUB = HGRN_CHUNK, HGRN_SUB
    Dh = HGRN_HEAD
    n_sub = C // SUB
    _, incl = _order_masks(d, C)
    g_end = jnp.sum(lf, axis=0, keepdims=True)
    W = q.shape[-1]
    anchors = [G[I * SUB + SUB // 2:I * SUB + SUB // 2 + 1, :] for I in range(n_sub)]
    g_anchor = jnp.concatenate([jnp.broadcast_to(a, (SUB, W)) for a in anchors], axis=0)
    q_a = q * jnp.exp(G - g_anchor)
    q_full = q * jnp.exp(G)
    k_hat = k * jnp.exp(g_end - G)
    sub_of_row = lax.broadcasted_iota(jnp.int32, (C, 1), 0) // SUB
    k_anch = []
    for I in range(n_sub):
        visible = (I - sub_of_row) * (1 - 2 * d) >= 0
        k_anch.append(k * jnp.exp(jnp.where(visible, anchors[I] - G, 0.0)))
    e_end = jnp.exp(g_end)
    out = dict(A=[], v=[], q_full=[], k_hat=[], e_end=[])
    for h in range(W // Dh):
        sl = slice(h * Dh, (h + 1) * Dh)
        A = jnp.concatenate([_dot_nt(q_a[I * SUB:(I + 1) * SUB, sl], k_anch[I][:, sl]) for I in range(n_sub)],
                            axis=0)
        out["A"].append(jnp.where(incl, A, 0.0))
        out["v"].append(v[:, sl])
        out["q_full"].append(q_full[:, sl])
        out["k_hat"].append(k_hat[:, sl])
        out["e_end"].append(e_end[:, sl])
    return out


def _hgrn_call(q, v, v_col, k, lf, n_ctx):
    B, N, W = q.shape
    T = HGRN_STEP
    shared, per_dir = _scan_specs(T, W, n_ctx, N, B)
    specs = [s for d in range(2) for s in (shared(d), shared(d, v_col), per_dir(d), per_dir(d))]
    return _scan_call(_hgrn_kernel, "hgrn2_scan", T, specs, (q, v, k, lf) * 2,
                      (2, W // HGRN_HEAD, HGRN_HEAD, HGRN_HEAD), n_ctx)


def _rope_tables(n, head):
    half = head // 2
    inv_freq = ROPE_BASE ** (-jnp.arange(half, dtype=F32) / half)
    ang = jnp.arange(n).astype(F32)[:, None] * inv_freq[None, :]
    cos, sin = jnp.cos(ang), jnp.sin(ang)
    return jnp.concatenate([cos, cos], axis=1), jnp.concatenate([-sin, sin], axis=1)


def _block_diag_ones(width, head):
    idx = jnp.arange(width) // head
    return (idx[:, None] == idx[None, :]).astype(BF16)


def _permute_w_in(w_in_l, v_down_l):
    D = w_in_l.shape[0]
    W = BRANCH_W
    n_rw, n_ret, n_hg = 3 * W + LORA_W, 4 * W, 5 * W
    rw = w_in_l[:, :n_rw]
    ret = w_in_l[:, n_rw:n_rw + n_ret]
    hg = w_in_l[:, n_rw + n_ret:n_rw + n_ret + n_hg]
    gate = w_in_l[:, n_rw + n_ret + n_hg:]
    extra = jnp.zeros((D, VDOWN_PAD), F32)
    if v_down_l is not None:
        extra = extra.at[:, :v_down_l.shape[1]].set(v_down_l)
    main = jnp.concatenate([gate, rw[:, :3 * W], ret, hg[:, :W], hg[:, 3 * W:]], axis=1)
    decay = jnp.concatenate([rw[:, 3 * W:], extra, hg[:, W:3 * W]], axis=1)
    assert main.shape[1] == MAIN_WIDTH and decay.shape[1] == DECAY_WIDTH
    return main.astype(BF16), decay.astype(BF16)


def kernel(x, c, ctx, c_ctx, ada_w, ada_b, norm1_w, norm2_w, w_in, rwkv_mu, rwkv_w0, rwkv_w_up, rwkv_a0, rwkv_a_up, rwkv_g_up, rwkv_k_k, rwkv_k_a, rwkv_r_k, rwkv_lnx_w, rwkv_lnx_b, rwkv_v0, rwkv_v_down, rwkv_v_up, ret_decay, hgrn_lb, hgrn_norm_w, w_branch, w_out, mlp_w1, mlp_w2, final_norm_w):
    B, SEQ, D = x.shape
    n_ctx = ctx.shape[1]
    N = n_ctx + SEQ
    L = ada_w.shape[0]
    W = BRANCH_W
    assert n_ctx == ROW_TILE and SEQ % ROW_TILE == 0 and D == D_MODEL
    assert MAIN_COLS["rwkv"] % (3 * W) == 0 and MAIN_COLS["ret"] % W == 0

    xa = jnp.concatenate([ctx, x], axis=1)
    cvec = jnp.concatenate([c, c_ctx[None], jnp.zeros((8 - B - 1, D), F32)], axis=0)
    mod = _ada_call(cvec, ada_w, ada_b)
    sm = jax.nn.softmax(hgrn_lb.astype(F32), axis=0)
    hgrn_lower = jnp.cumsum(sm, axis=0) - sm[0:1]
    tables = _rope_tables(N, RET_HEAD)
    ones64 = _block_diag_ones(W, RWKV_HEAD)
    ones128 = _block_diag_ones(W, RET_HEAD)
    row = lambda t: t.reshape(1, -1)
    v_first = None
    out = None

    for l in range(L):
        last = l == L - 1
        mod_c = jnp.broadcast_to(mod[l, B].reshape(1, 6, D), (B, 6, D))
        modsel = jnp.concatenate([mod_c, mod[l, :B].reshape(B, 6, D)], axis=1)
        w_main, w_decay = _permute_w_in(w_in[l], None if l == 0 else rwkv_v_down[l - 1])
        P = _proj_call(xa, modsel, norm1_w[l], w_main, n_ctx, BF16, 4)
        Pd = _proj_call(xa, modsel, norm1_w[l], w_decay, n_ctx, F32, 1)

        has_vres = l > 0
        if has_vres:
            v0 = row(rwkv_v0[l - 1])
            v_up = jnp.zeros((VDOWN_PAD, W), F32).at[:rwkv_v_up.shape[1]].set(rwkv_v_up[l - 1]).astype(BF16)
        else:
            v0 = jnp.zeros((1, W), F32)
            v_up = jnp.zeros((VDOWN_PAD, W), BF16)
        prep_w = (row(rwkv_mu[l, :3 * W]), row(rwkv_mu[l, 3 * W:]),
                  rwkv_w_up[l].astype(BF16), rwkv_w0[l], rwkv_a_up[l].astype(BF16), rwkv_a0[l],
                  rwkv_g_up[l].astype(BF16), row(rwkv_k_k[l]), row(rwkv_k_a[l]), row(rwkv_r_k[l]),
                  v0, v_up, ones64, hgrn_lower[l])
        (r, v, kk, lw, kd, a, g, bonus, rq, rk, hq, hk, hlf) = _prep_call(
            P, Pd, P if v_first is None else v_first, tables, prep_w, n_ctx, has_vres)
        if v_first is None:
            v_first = v

        o_rw_dirs = _rwkv_call(r, v, kk, lw, kd, a, n_ctx)
        log_gamma = -jnp.exp(ret_decay[l].astype(F32))
        lg = jnp.repeat(log_gamma, RET_HEAD, axis=-1).reshape(2, 1, W)
        o_ret_dirs = _ret_call(rq, rk, P, (MAIN_COLS["ret"] + 2 * W) // W, lg, n_ctx)
        o_hg_dirs = _hgrn_call(hq, P, MAIN_COLS["hgrn_i"] // W, hk, hlf, n_ctx)

        merge_w = (row(rwkv_lnx_w[l]), row(rwkv_lnx_b[l]), row(hgrn_norm_w[l]), ones64, ones128,
                   w_branch[l].astype(BF16), w_out[l].astype(BF16))
        xm = _merge_call(xa, P, o_rw_dirs, g, bonus, o_ret_dirs, o_hg_dirs, merge_w, modsel, n_ctx, last)
        xa = _mlp_call(xm, modsel, norm2_w[l], mlp_w1[l].astype(BF16), mlp_w2[l].astype(BF16),
                       final_norm_w, n_ctx, (n_ctx // ROW_TILE) if last else 0, last)
        out = xa
    return out
```

```python
import functools

import jax
import jax.numpy as jnp
import numpy as np
from jax import lax
from jax.experimental import pallas as pl
from jax.experimental.pallas import tpu as pltpu

F32 = jnp.float32
BF16 = jnp.bfloat16

NORM_EPS = 1e-6
RWKV_LN_EPS = 64e-5
DECAY_SCALE = 0.6065306597126334
L2_EPS = 1e-12
ROPE_BASE = 10000.0
GRID_W = 64

BRANCH_W = 512
RWKV_HEAD = 64
RET_HEAD = 128
HGRN_HEAD = 128
DECAY_LORA = 64
AAA_LORA = 64
GATE_LORA = 128
LORA_W = 2 * DECAY_LORA + 2 * AAA_LORA + GATE_LORA
VDOWN_PAD = 128

RWKV_CHUNK = 64
RWKV_STEP = 128
HGRN_CHUNK = 64
HGRN_SUB = 32
HGRN_STEP = 128
RET_CHUNK = 128
ROW_TILE = 256

VMEM_LIMIT = 56 * 1024 * 1024

D_MODEL = 2 * BRANCH_W
MAIN_COLS = dict(gate=0, rwkv=3 * D_MODEL, ret=3 * D_MODEL + 3 * BRANCH_W, hgrn_q=3 * D_MODEL + 7 * BRANCH_W,
                 hgrn_i=3 * D_MODEL + 8 * BRANCH_W, hgrn_g=3 * D_MODEL + 9 * BRANCH_W)
MAIN_WIDTH = 3 * D_MODEL + 10 * BRANCH_W
DECAY_COLS = dict(lora=0, v_down=LORA_W, hgrn_f=LORA_W + VDOWN_PAD)
DECAY_WIDTH = LORA_W + VDOWN_PAD + 2 * BRANCH_W


def _sigmoid(x):
    return 1.0 / (1.0 + jnp.exp(-x))


def _dot(a, b):
    return jnp.dot(a.astype(BF16), b.astype(BF16), preferred_element_type=F32)


def _dot_nt(a, b):
    return lax.dot_general(a.astype(BF16), b.astype(BF16), (((1,), (1,)), ((), ())),
                           preferred_element_type=F32)


def _dot_tn(a, b):
    return lax.dot_general(a.astype(BF16), b.astype(BF16), (((0,), (0,)), ((), ())),
                           preferred_element_type=F32)


def _cumsum_rows(mask_f32, x):
    m = mask_f32.astype(BF16)
    hi = x.astype(BF16)
    lo = (x - hi.astype(F32)).astype(BF16)
    return jnp.dot(m, hi, preferred_element_type=F32) + jnp.dot(m, lo, preferred_element_type=F32)


def _head_sum(x, ones_bd, split=True):
    hi = x.astype(BF16)
    out = jnp.dot(hi, ones_bd, preferred_element_type=F32)
    if split:
        lo = (x - hi.astype(F32)).astype(BF16)
        out = out + jnp.dot(lo, ones_bd, preferred_element_type=F32)
    return out


def _ada_kernel(c_ref, w_ref, b_ref, o_ref):
    cv = c_ref[...]
    s = cv * _sigmoid(cv)
    o_ref[0] = _dot(s, w_ref[0]) + b_ref[0]


def _ada_call(cvec, ada_w, ada_b):
    L, D, D6 = ada_w.shape
    tn = D6 // 4
    return pl.pallas_call(
        _ada_kernel,
        grid=(L, D6 // tn),
        in_specs=[pl.BlockSpec((8, D), lambda l, j: (0, 0)),
                  pl.BlockSpec((1, D, tn), lambda l, j: (l, 0, j)),
                  pl.BlockSpec((1, 1, tn), lambda l, j: (l, 0, j))],
        out_specs=pl.BlockSpec((1, 8, tn), lambda l, j: (l, 0, j)),
        out_shape=jax.ShapeDtypeStruct((L, 8, D6), F32),
        compiler_params=pltpu.CompilerParams(vmem_limit_bytes=VMEM_LIMIT),
        name="ada_mod",
    )(cvec, ada_w, ada_b.reshape(L, 1, D6))


def _modulated_norm(xf, nw, mod_ref, is_ctx, which):
    y = xf * lax.rsqrt(jnp.mean(xf * xf, axis=-1, keepdims=True) + NORM_EPS) * nw
    sh = jnp.where(is_ctx, mod_ref[0, which:which + 1, :], mod_ref[0, 6 + which:7 + which, :])
    sc = jnp.where(is_ctx, mod_ref[0, which + 1:which + 2, :], mod_ref[0, 7 + which:8 + which, :])
    return y * (1.0 + sc) + sh


def _is_ctx_rows(tile_idx, tm, n_ctx):
    row = tile_idx * tm + lax.broadcasted_iota(jnp.int32, (tm, 1), 0)
    return row < n_ctx


def _proj_kernel(x_ref, mod_ref, nw_ref, w_ref, o_ref, h_ref, *, tm, n_ctx):
    i = pl.program_id(1)

    @pl.when(pl.program_id(2) == 0)
    def _():
        h = _modulated_norm(x_ref[0], nw_ref[...], mod_ref, _is_ctx_rows(i, tm, n_ctx), 0)
        h_ref[...] = h.astype(BF16)

    o_ref[0] = jnp.dot(h_ref[...], w_ref[...], preferred_element_type=F32).astype(o_ref.dtype)


def _proj_call(xa, modsel, nw, w, n_ctx, out_dtype, n_tiles, tm=768):
    B, N, D = xa.shape
    NP = w.shape[1]
    tn = NP // n_tiles
    return pl.pallas_call(
        functools.partial(_proj_kernel, tm=tm, n_ctx=n_ctx),
        grid=(B, N // tm, n_tiles),
        in_specs=[pl.BlockSpec((1, tm, D), lambda b, i, j: (b, i, 0)),
                  pl.BlockSpec((1, 12, D), lambda b, i, j: (b, 0, 0)),
                  pl.BlockSpec((1, D), lambda b, i, j: (0, 0)),
                  pl.BlockSpec((D, tn), lambda b, i, j: (0, j))],
        out_specs=pl.BlockSpec((1, tm, tn), lambda b, i, j: (b, i, j)),
        out_shape=jax.ShapeDtypeStruct((B, N, NP), out_dtype),
        scratch_shapes=[pltpu.VMEM((tm, D), BF16)],
        compiler_params=pltpu.CompilerParams(
            dimension_semantics=("parallel", "parallel", "arbitrary"),
            vmem_limit_bytes=VMEM_LIMIT),
        name="in_proj",
    )(xa, modsel, nw.reshape(1, D), w)


def _shift_matrices(tm, seq):
    mats = np.zeros((5, tm, tm + 2 * GRID_W), np.float32)
    t = np.arange(tm)
    centre = t + GRID_W
    mats[0, t[1:], centre[1:] - 1] = 0.5
    mats[0, t[:-1], centre[:-1] + 1] = 0.5
    for kind, (first, last) in enumerate(((True, False), (False, False), (False, True), (True, True)), start=1):
        has_left = t % GRID_W != 0
        has_right = t % GRID_W != GRID_W - 1
        has_up = ~(first & (t < GRID_W))
        has_down = ~(last & (t >= tm - GRID_W))
        mats[kind, t[has_left], centre[has_left] - 1] = 0.25
        mats[kind, t[has_right], centre[has_right] + 1] = 0.25
        mats[kind, t[has_up], centre[has_up] - GRID_W] = 0.25
        mats[kind, t[has_down], centre[has_down] + GRID_W] = 0.25
    n_lat = seq // tm

    def kind_of_tile(i):
        latent = jnp.where(i == 1, 4 if n_lat == 1 else 1, jnp.where(i == n_lat, 3, 2))
        return jnp.where(i == 0, 0, latent)

    return jnp.asarray(mats, BF16), kind_of_tile


def _shift_mix(cur, prev, nxt, mu, s):
    ext = jnp.concatenate([prev, cur, nxt], axis=0)
    if ext.dtype == BF16:
        shifted = jnp.dot(s, ext, preferred_element_type=F32)
    else:
        hi = ext.astype(BF16)
        lo = (ext - hi.astype(F32)).astype(BF16)
        shifted = jnp.dot(s, hi, preferred_element_type=F32) + jnp.dot(s, lo, preferred_element_type=F32)
    cur = cur.astype(F32)
    return cur + mu * (shifted - cur)


def _prep_kernel(pc_ref, pp_ref, pn_ref, lc_ref, lp_ref, ln_ref, hv_ref, vf_ref, rq_ref, rk_ref, cs_ref, sn_ref,
                 hq_ref, hf0_ref, hf1_ref,
                 mum_ref, mul_ref, wup_ref, w0_ref, aup_ref, a0_ref, gup_ref, kk_w_ref, ka_ref, r_k_ref,
                 v0_ref, vup_ref, ones_ref, lb_ref,
                 shift_ref,
                 r_o, v_o, kk_o, lw_o, kd_o, a_o, g_o, bonus_o, rq_o, rk_o, hq_o, hk_o, hlf_o,
                 *, has_vres):
    W = BRANCH_W

    p = _shift_mix(pc_ref[0], pp_ref[0], pn_ref[0], mum_ref[...], shift_ref[0])
    lo = _shift_mix(lc_ref[0], lp_ref[0], ln_ref[0], mul_ref[...], shift_ref[0])
    r, k, v = p[:, :W], p[:, W:2 * W], p[:, 2 * W:]
    if has_vres:
        mix = _sigmoid(v0_ref[...] + _dot(hv_ref[0], vup_ref[...]))
        v = v + (vf_ref[0].astype(F32) - v) * mix
    r_o[0] = r.astype(r_o.dtype)
    v_o[0] = v.astype(v_o.dtype)
    gd = lo[:, 2 * DECAY_LORA + 2 * AAA_LORA:]
    g_o[0] = _dot(_sigmoid(gd), gup_ref[...]).astype(g_o.dtype)
    kx = k * kk_w_ref[...]
    kk_o[0] = (kx * lax.rsqrt(_head_sum(kx * kx, ones_ref[...], split=False) + L2_EPS)).astype(kk_o.dtype)
    k_sum = None
    for d in range(2):
        wd = lo[:, d * DECAY_LORA:(d + 1) * DECAY_LORA]
        ad = lo[:, 2 * DECAY_LORA + d * AAA_LORA:2 * DECAY_LORA + (d + 1) * AAA_LORA]
        z = -(w0_ref[d:d + 1, :] + _dot(jnp.tanh(wd), wup_ref[d]))
        lw_o[d, 0] = -DECAY_SCALE / (1.0 + jnp.exp(z))
        a = _sigmoid(a0_ref[d:d + 1, :] + _dot(ad, aup_ref[d]))
        a_o[d, 0] = a.astype(a_o.dtype)
        kd = k * (1.0 + (a - 1.0) * ka_ref[...])
        kd_o[d, 0] = kd.astype(kd_o.dtype)
        k_sum = kd if k_sum is None else k_sum + kd
    bonus_o[0] = (_head_sum(r * k_sum * r_k_ref[...], ones_ref[...], split=False) * v).astype(bonus_o.dtype)

    cosf = jnp.concatenate([cs_ref[...]] * (W // RET_HEAD), axis=1)
    sinf = jnp.concatenate([sn_ref[...]] * (W // RET_HEAD), axis=1)
    lane = lax.broadcasted_iota(jnp.int32, (1, W), 1)
    first_half = (lane & (RET_HEAD - 1)) < RET_HEAD // 2

    def rope(t):
        rot = jnp.where(first_half, pltpu.roll(t, W - RET_HEAD // 2, 1), pltpu.roll(t, RET_HEAD // 2, 1))
        return t * cosf + rot * sinf

    rq_o[0] = rope(rq_ref[0].astype(F32)).astype(rq_o.dtype)
    rk_o[0] = (rope(rk_ref[0].astype(F32)) * (RET_HEAD ** -0.5)).astype(rk_o.dtype)

    q = hq_ref[0].astype(F32)
    hq_o[0] = (q * _sigmoid(q)).astype(hq_o.dtype)
    for d, f_ref in enumerate((hf0_ref, hf1_ref)):
        f = f_ref[0]
        e = jnp.exp(-jnp.abs(f))
        inv = 1.0 / (1.0 + e)
        sig_pos = jnp.where(f >= 0, inv, e * inv)
        sig_neg = jnp.where(f >= 0, e * inv, inv)
        lb = lb_ref[d:d + 1, :]
        hlf_o[d, 0] = jnp.log(lb + (1.0 - lb) * sig_pos)
        hk_o[d, 0] = ((1.0 - lb) * sig_neg).astype(hk_o.dtype)


def _prep_call(P, Pd, v_first, tables, wts, n_ctx, has_vres):
    B, N, _ = P.shape
    W = BRANCH_W
    tm = ROW_TILE
    seq = N - n_ctx
    halo_per_tile = tm // GRID_W
    n_halo = N // GRID_W
    c_main = MAIN_COLS["rwkv"] // (3 * W)
    c_rq = MAIN_COLS["ret"] // W
    c_hq = MAIN_COLS["hgrn_q"] // W
    c_lora = DECAY_COLS["lora"] // LORA_W
    c_vd = DECAY_COLS["v_down"] // VDOWN_PAD
    c_hf = DECAY_COLS["hgrn_f"] // W

    def cur(width, col):
        return pl.BlockSpec((1, tm, width), lambda b, i: (b, i, col))

    def prev(width, col):
        return pl.BlockSpec((1, GRID_W, width), lambda b, i: (b, jnp.maximum(i * halo_per_tile - 1, 0), col))

    def nxt(width, col):
        return pl.BlockSpec((1, GRID_W, width),
                            lambda b, i: (b, jnp.minimum((i + 1) * halo_per_tile, n_halo - 1), col))

    def const(shape):
        return pl.BlockSpec(shape, lambda b, i: (0,) * len(shape))

    tok = pl.BlockSpec((1, tm, W), lambda b, i: (b, i, 0))
    tok2 = pl.BlockSpec((2, 1, tm, W), lambda b, i: (0, b, i, 0))
    cos, sin = tables
    in_specs = [cur(3 * W, c_main), prev(3 * W, c_main), nxt(3 * W, c_main),
                cur(LORA_W, c_lora), prev(LORA_W, c_lora), nxt(LORA_W, c_lora),
                cur(VDOWN_PAD, c_vd), tok, cur(W, c_rq), cur(W, c_rq + 1),
                pl.BlockSpec((tm, RET_HEAD), lambda b, i: (i, 0)),
                pl.BlockSpec((tm, RET_HEAD), lambda b, i: (i, 0)),
                cur(W, c_hq), cur(W, c_hf), cur(W, c_hf + 1)]
    in_specs += [const(w.shape) for w in wts]
    shift_mats, kind_of_tile = _shift_matrices(tm, seq)
    in_specs.append(pl.BlockSpec((1,) + shift_mats.shape[1:], lambda b, i: (kind_of_tile(i), 0, 0)))
    one = jax.ShapeDtypeStruct((B, N, W), BF16)
    two = jax.ShapeDtypeStruct((2, B, N, W), BF16)
    two_f32 = jax.ShapeDtypeStruct((2, B, N, W), F32)
    out_shape = [one, one, one, two_f32, two, two, one, one, one, one, one, two, two_f32]
    out_specs = [tok, tok, tok, tok2, tok2, tok2, tok, tok, tok, tok, tok, tok2, tok2]
    return pl.pallas_call(
        functools.partial(_prep_kernel, has_vres=has_vres),
        grid=(B, N // tm),
        in_specs=in_specs,
        out_specs=out_specs,
        out_shape=out_shape,
        compiler_params=pltpu.CompilerParams(
            dimension_semantics=("parallel", "parallel"), vmem_limit_bytes=VMEM_LIMIT),
        name="branch_prep",
    )(P, P, P, Pd, Pd, Pd, Pd, v_first, P, P, cos, sin, P, Pd, Pd, *wts, shift_mats)


def _merge_kernel(x_ref, g0_ref, g1_ref, g2_ref, orwf_ref, orwb_ref, g_ref, bonus_ref, oretf_ref, oretb_ref,
                  rg_ref, ohgf_ref, ohgb_ref, hg_ref,
                  lnw_ref, lnb_ref, hnw_ref, m64_ref, m128_ref, wb_ref, wo_ref, mod_ref, o_ref,
                  *, tm, n_ctx, tile_off):
    i = pl.program_id(1) + tile_off

    o = orwf_ref[0].astype(F32) + orwb_ref[0].astype(F32)
    cen = o - _head_sum(o, m64_ref[...]) * (1.0 / RWKV_HEAD)
    var = _head_sum(cen * cen, m64_ref[...], split=False) * (1.0 / RWKV_HEAD)
    y_rw = ((cen * lax.rsqrt(var + RWKV_LN_EPS) * lnw_ref[...] + lnb_ref[...] + bonus_ref[0].astype(F32))
            * g_ref[0].astype(F32))
    o = oretf_ref[0].astype(F32) + oretb_ref[0].astype(F32)
    ms = _head_sum(o * o, m128_ref[...], split=False) * (1.0 / RET_HEAD)
    rg = rg_ref[0].astype(F32)
    y_ret = o * lax.rsqrt(ms + NORM_EPS) * (rg * _sigmoid(rg))
    o = ohgf_ref[0].astype(F32) + ohgb_ref[0].astype(F32)
    ms = _head_sum(o * o, m128_ref[...], split=False) * (1.0 / HGRN_HEAD)
    hg = hg_ref[0].astype(F32)
    y_hg = o * lax.rsqrt(ms + NORM_EPS) * hnw_ref[...] * (hg * _sigmoid(hg))

    merged = None
    for b, (y, gate_ref) in enumerate(((y_rw, g0_ref), (y_ret, g1_ref), (y_hg, g2_ref))):
        z = _sigmoid(gate_ref[0].astype(F32)) * _dot(y, wb_ref[b])
        merged = z if merged is None else merged + z
    out = _dot(merged, wo_ref[...])
    g1 = jnp.where(_is_ctx_rows(i, tm, n_ctx), mod_ref[0, 2:3, :], mod_ref[0, 8:9, :])
    o_ref[0] = x_ref[0] + g1 * out


def _merge_call(xa, P, o_rw, g, bonus, o_ret, o_hg, wts, modsel, n_ctx, latent_only):
    B, N, D = xa.shape
    W = BRANCH_W
    tm = ROW_TILE
    off = n_ctx // tm if latent_only else 0
    n_rows = N - off * tm
    c_gate = MAIN_COLS["gate"] // D
    c_rg = (MAIN_COLS["ret"] + 3 * W) // W
    c_hg = MAIN_COLS["hgrn_g"] // W

    def col(width, c):
        return pl.BlockSpec((1, tm, width), lambda b, i: (b, i + off, c))

    def const(shape):
        return pl.BlockSpec(shape, lambda b, i: (0,) * len(shape))

    tok = col(W, 0)
    return pl.pallas_call(
        functools.partial(_merge_kernel, tm=tm, n_ctx=n_ctx, tile_off=off),
        grid=(B, n_rows // tm),
        in_specs=[col(D, 0), col(D, c_gate), col(D, c_gate + 1), col(D, c_gate + 2),
                  tok, tok, tok, tok, tok, tok, col(W, c_rg), tok, tok, col(W, c_hg)]
                 + [const(w.shape) for w in wts]
                 + [pl.BlockSpec((1, 12, D), lambda b, i: (b, 0, 0))],
        out_specs=pl.BlockSpec((1, tm, D), lambda b, i: (b, i, 0)),
        out_shape=jax.ShapeDtypeStruct((B, n_rows, D), F32),
        compiler_params=pltpu.CompilerParams(
            dimension_semantics=("parallel", "parallel"), vmem_limit_bytes=VMEM_LIMIT),
        name="merge_out",
    )(xa, P, P, P, o_rw[0], o_rw[1], g, bonus, o_ret[0], o_ret[1], P, o_hg[0], o_hg[1], P, *wts, modsel)


def _mlp_kernel(x_ref, mod_ref, nw_ref, w1_ref, w2_ref, fw_ref, o_ref, *, tm, n_ctx, tile_off, final):
    i = pl.program_id(1) + tile_off
    is_ctx = _is_ctx_rows(i, tm, n_ctx)
    xf = x_ref[0]
    h = _modulated_norm(xf, nw_ref[...], mod_ref, is_ctx, 3)
    a = jnp.maximum(_dot(h, w1_ref[...]), 0.0)
    out = _dot(a * a, w2_ref[...])
    g2 = jnp.where(is_ctx, mod_ref[0, 5:6, :], mod_ref[0, 11:12, :])
    xn = xf + g2 * out
    if final:
        xn = xn * lax.rsqrt(jnp.mean(xn * xn, axis=-1, keepdims=True) + NORM_EPS) * fw_ref[...]
    o_ref[0] = xn


def _mlp_call(xa, modsel, nw, w1, w2, fw, n_ctx, tile_off, final):
    B, n_rows, D = xa.shape
    DF = w1.shape[1]
    tm = ROW_TILE
    return pl.pallas_call(
        functools.partial(_mlp_kernel, tm=tm, n_ctx=n_ctx, tile_off=tile_off, final=final),
        grid=(B, n_rows // tm),
        in_specs=[pl.BlockSpec((1, tm, D), lambda b, i: (b, i, 0)),
                  pl.BlockSpec((1, 12, D), lambda b, i: (b, 0, 0)),
                  pl.BlockSpec((1, D), lambda b, i: (0, 0)),
                  pl.BlockSpec((D, DF), lambda b, i: (0, 0)),
                  pl.BlockSpec((DF, D), lambda b, i: (0, 0)),
                  pl.BlockSpec((1, D), lambda b, i: (0, 0))],
        out_specs=pl.BlockSpec((1, tm, D), lambda b, i: (b, i, 0)),
        out_shape=jax.ShapeDtypeStruct((B, n_rows, D), F32),
        compiler_params=pltpu.CompilerParams(
            dimension_semantics=("parallel", "parallel"), vmem_limit_bytes=VMEM_LIMIT),
        name="mlp",
    )(xa, modsel, nw.reshape(1, D), w1, w2, fw.reshape(1, D))


def _order_masks(d, n):
    t = lax.broadcasted_iota(jnp.int32, (n, n), 0)
    s = lax.broadcasted_iota(jnp.int32, (n, n), 1)
    lead = (t - s) * (1 - 2 * d)
    return lead > 0, lead >= 0


def _scan_specs(T, W, n_ctx, N, B):
    n_ctx_blocks, n_blocks = n_ctx // T, N // T

    def block(d, j):
        if d == 0:
            return j
        return jnp.where(j < n_ctx_blocks, n_ctx_blocks - 1 - j, n_blocks - 1 + n_ctx_blocks - j)

    def shared(d, col=0):
        return pl.BlockSpec((B, T, W), lambda j: (0, block(d, j), col))

    def per_dir(d):
        return pl.BlockSpec((1, B, T, W), lambda j: (d, 0, block(d, j), 0))

    return shared, per_dir


def _scan_call(kernel_fn, name, T, specs, args, state_shape, n_ctx):
    B, N, W = args[0].shape[0], args[0].shape[1], BRANCH_W
    shared, _ = _scan_specs(T, W, n_ctx, N, B)
    out = jax.ShapeDtypeStruct((B, N, W), BF16)
    return pl.pallas_call(
        kernel_fn,
        grid=(N // T,),
        in_specs=specs,
        out_specs=[shared(0), shared(1)],
        out_shape=[out, out],
        scratch_shapes=[pltpu.VMEM((B,) + state_shape, F32)],
        compiler_params=pltpu.CompilerParams(
            dimension_semantics=("arbitrary",), vmem_limit_bytes=VMEM_LIMIT),
        name=name,
    )(*args)


def _load(ref, b, rows=slice(None)):
    x = ref[b, rows, :] if len(ref.shape) == 3 else ref[0, b, rows, :]
    return x.astype(F32)


def _rwkv_kernel(*refs):
    sides, (of_ref, ob_ref, s_ref) = (refs[0:6], refs[6:12]), refs[12:]
    n_batch, _, n_groups = s_ref.shape[:3]

    @pl.when(pl.program_id(0) == 0)
    def _():
        s_ref[...] = jnp.zeros_like(s_ref)

    C = RWKV_CHUNK
    n_sub = RWKV_STEP // C
    seqs = [(b, d) for b in range(n_batch) for d in range(2)]
    S = {(b, d, g): s_ref[b, d, g] for b, d in seqs for g in range(n_groups)}
    for i in range(n_sub):
        chunk = (i, n_sub - 1 - i)
        rows = [slice(c * C, (c + 1) * C) for c in chunk]
        ins = {(b, d): tuple(_load(ref, b, rows[d]) for ref in sides[d]) for b, d in seqs}
        outs, S = _rwkv_step(seqs, ins, S, n_groups)
        for b, d in seqs:
            (of_ref, ob_ref)[d][b, rows[d], :] = outs[b, d].astype(BF16)
    for key, s_new in S.items():
        s_ref[key] = s_new


def _rwkv_step(seqs, ins, S, n_groups):
    C = RWKV_CHUNK
    HG = 4 * RWKV_HEAD
    chains = [sq + (g,) for sq in seqs for g in range(n_groups)]

    rb = lax.broadcasted_iota(jnp.int32, (4 * C, HG), 0) // C
    lb = lax.broadcasted_iota(jnp.int32, (4 * C, HG), 1) // RWKV_HEAD
    same_head = rb == lb
    head_mask = same_head.astype(F32).astype(BF16)
    t_idx = lax.broadcasted_iota(jnp.int32, (C, HG), 0)
    s_idx = lax.broadcasted_iota(jnp.int32, (C, HG), 1) % C

    def bd(x):
        xb = x.astype(BF16)
        return jnp.concatenate([xb, xb, xb, xb], axis=0) * head_mask

    incl_c = [_order_masks(d, C)[1].astype(F32) for d in range(2)]
    G = {sq: _cumsum_rows(incl_c[sq[1]], ins[sq][3]) for sq in seqs}
    strict, incl, eye = {}, {}, {}
    for d in range(2):
        lead = (t_idx - s_idx) * (1 - 2 * d)
        strict[d], incl[d], eye[d] = lead > 0, lead >= 0, (lead == 0).astype(F32)
    q_t, r_t, p_t, k_t, p_hat, k_hat, v_g, e_end = {}, {}, {}, {}, {}, {}, {}, {}
    for sq in seqs:
        r, v, kk, lw, k, a = ins[sq]
        g_end = jnp.sum(lw, axis=0, keepdims=True)
        e_neg = jnp.exp(-G[sq])
        end = jnp.exp(g_end)
        full = dict(q=kk * jnp.exp(G[sq] - lw), r=r * jnp.exp(G[sq]), p=-(a * kk) * e_neg, k=k * e_neg, v=v)
        for g in range(n_groups):
            sl = slice(g * HG, (g + 1) * HG)
            c = sq + (g,)
            q_t[c], r_t[c], p_t[c], k_t[c], v_g[c] = (full[n][:, sl] for n in "qrpkv")
            e_end[c] = end[:, sl]
            p_hat[c], k_hat[c] = p_t[c] * e_end[c], k_t[c] * e_end[c]

    bv = {c: bd(v_g[c]) for c in chains}
    qr = {c: jnp.concatenate([q_t[c], r_t[c]], axis=0).astype(BF16) for c in chains}
    a_p = {c: _dot_nt(qr[c], bd(p_t[c])) for c in chains}
    a_k = {c: _dot_nt(qr[c], bd(k_t[c])) for c in chains}
    n_qp = {c: jnp.where(strict[c[1]], a_p[c][:C], 0.0) for c in chains}
    a_qk = {c: jnp.where(strict[c[1]], a_k[c][:C], 0.0) for c in chains}
    a_rp = {c: jnp.where(incl[c[1]], a_p[c][C:], 0.0) for c in chains}
    a_rk = {c: jnp.where(incl[c[1]], a_k[c][C:], 0.0) for c in chains}

    R = {c: _dot(n_qp[c], bd(n_qp[c])) for c in chains}
    T = {c: eye[c[1]] + n_qp[c] for c in chains}
    avk = {c: _dot(jnp.concatenate([a_qk[c], a_rk[c]], axis=0), bv[c]) for c in chains}
    span = 4
    while span < C:
        RT = {c: _dot(jnp.concatenate([R[c], T[c]], axis=0), bd(R[c])) for c in chains}
        R = {c: RT[c][:C] for c in chains}
        T = {c: T[c] + RT[c][C:] for c in chains}
        span *= 2
    TR = {c: _dot(T[c], bd(R[c])) for c in chains}
    Tb = {c: (T[c] + TR[c]).astype(BF16) for c in chains}

    Sb = {c: S[c].astype(BF16) for c in chains}
    w = {c: _dot(Tb[c], bd(q_t[c])) for c in chains}
    u0 = {c: _dot(Tb[c], bd(avk[c][:C])) for c in chains}
    ws = {c: _dot_nt(jnp.concatenate([w[c], r_t[c]], axis=0), Sb[c]) for c in chains}
    u = {c: ws[c][:C] + u0[c] for c in chains}
    o = {c: _dot(a_rp[c], bd(u[c])) + avk[c][C:] + ws[c][C:] for c in chains}
    upd = {c: _dot_tn(jnp.concatenate([u[c], v_g[c]], axis=0),
                      jnp.concatenate([p_hat[c], k_hat[c]], axis=0)) for c in chains}
    S_new = {c: S[c] * e_end[c] + jnp.where(same_head, upd[c], 0.0) for c in chains}
    outs = {sq: jnp.concatenate([o[sq + (g,)] for g in range(n_groups)], axis=1) for sq in seqs}
    return outs, S_new


def _rwkv_call(r, v, kk, lw, k, a, n_ctx):
    B, N, W = r.shape
    T = RWKV_STEP
    assert RWKV_CHUNK == RWKV_HEAD and W % (4 * RWKV_HEAD) == 0
    shared, per_dir = _scan_specs(T, W, n_ctx, N, B)
    specs = [s for d in range(2) for s in (shared(d), shared(d), shared(d), per_dir(d), per_dir(d), per_dir(d))]
    args = (r, v, kk, lw, k, a) * 2
    n_groups = W // (4 * RWKV_HEAD)
    return _scan_call(_rwkv_kernel, "rwkv7_scan", T, specs, args,
                      (2, n_groups, 4 * RWKV_HEAD, 4 * RWKV_HEAD), n_ctx)


def _ret_kernel(qf_ref, kf_ref, vf_ref, qb_ref, kb_ref, vb_ref, lg_ref, of_ref, ob_ref, s_ref):
    C = RET_CHUNK
    Dh = RET_HEAD

    @pl.when(pl.program_id(0) == 0)
    def _():
        s_ref[...] = jnp.zeros_like(s_ref)

    t = lax.broadcasted_iota(jnp.int32, (C, C), 0)
    s = lax.broadcasted_iota(jnp.int32, (C, C), 1)
    dist = jnp.abs(t - s).astype(F32)
    tcol = lax.broadcasted_iota(jnp.int32, (C, 1), 0)
    io = ((qf_ref, kf_ref, vf_ref, of_ref), (qb_ref, kb_ref, vb_ref, ob_ref))
    n_batch = s_ref.shape[0]
    n_heads = BRANCH_W // Dh
    chains = [(b, d, h) for b in range(n_batch) for d in range(2) for h in range(n_heads)]
    incl = [_order_masks(d, C)[1] for d in range(2)]
    n_t = [(tcol + 1 if d == 0 else C - tcol).astype(F32) for d in range(2)]

    def head(ref, b, h):
        return ref[b, :, h * Dh:(h + 1) * Dh].astype(F32)

    scores = {(b, d, h): _dot_nt(head(io[d][0], b, h), head(io[d][1], b, h)) for b, d, h in chains}
    outs, H = {}, {}
    for b, d, h in chains:
        lg = lg_ref[d, :, h * Dh:h * Dh + C]
        lgd = lg_ref[d, :, h * Dh:(h + 1) * Dh]
        A = scores[b, d, h] * jnp.where(incl[d], jnp.exp(lg * dist), 0.0)
        H[b, d, h] = s_ref[b, d, h]
        outs[b, d, h] = (_dot(A, head(io[d][2], b, h))
                         + _dot(head(io[d][0], b, h) * jnp.exp(lgd * n_t[d]), H[b, d, h]))
    for b, d, h in chains:
        lgd = lg_ref[d, :, h * Dh:(h + 1) * Dh]
        k_hat = head(io[d][1], b, h) * jnp.exp(lgd * (C - n_t[d]))
        s_ref[b, d, h] = H[b, d, h] * jnp.exp(lgd * C) + _dot_tn(k_hat, head(io[d][2], b, h))
    for b in range(n_batch):
        for d in range(2):
            io[d][3][b] = jnp.concatenate([outs[b, d, h] for h in range(n_heads)], axis=1).astype(BF16)


def _ret_call(q, k, v, v_col, lg, n_ctx):
    B, N, W = q.shape
    C = RET_CHUNK
    assert C <= RET_HEAD
    shared, _ = _scan_specs(C, W, n_ctx, N, B)
    specs = [s for d in range(2) for s in (shared(d), shared(d), shared(d, v_col))]
    specs.append(pl.BlockSpec((2, 1, W), lambda j: (0, 0, 0)))
    return _scan_call(_ret_kernel, "retention_scan", C, specs, (q, k, v, q, k, v, lg),
                      (2, W // RET_HEAD, RET_HEAD, RET_HEAD), n_ctx)


def _hgrn_kernel(qf_ref, vf_ref, kf_ref, lff_ref, qb_ref, vb_ref, kb_ref, lfb_ref, of_ref, ob_ref, s_ref):
    C = HGRN_CHUNK
    n_batch, _, n_heads = s_ref.shape[:3]

    @pl.when(pl.program_id(0) == 0)
    def _():
        s_ref[...] = jnp.zeros_like(s_ref)

    io = ((qf_ref, vf_ref, kf_ref, lff_ref, of_ref), (qb_ref, vb_ref, kb_ref, lfb_ref, ob_ref))
    n_chunks = HGRN_STEP // C
    order = [list(range(n_chunks)), list(reversed(range(n_chunks)))]
    seqs = [(b, d) for b in range(n_batch) for d in range(2)]
    blocks = [sq + (c,) for sq in seqs for c in range(n_chunks)]
    incl_c = [_order_masks(d, C)[1].astype(F32) for d in range(2)]

    def rows(ref, b, c):
        x = ref[b, c * C:(c + 1) * C, :] if len(ref.shape) == 3 else ref[0, b, c * C:(c + 1) * C, :]
        return x.astype(F32)

    G = {(b, d, c): _cumsum_rows(incl_c[d], rows(io[d][3], b, c)) for b, d, c in blocks}
    pre = {}
    for b, d, c in blocks:
        q_ref, v_ref, k_ref, lf_ref, _ = io[d]
        pre[b, d, c] = _hgrn_intra(d, rows(q_ref, b, c), rows(k_ref, b, c), rows(v_ref, b, c),
                                   rows(lf_ref, b, c), G[b, d, c])
    heads = range(n_heads)
    intra = {blk + (h,): _dot(pre[blk]["A"][h], pre[blk]["v"][h]) for blk in blocks for h in heads}

    S = {sq + (h,): s_ref[sq + (h,)] for sq in seqs for h in heads}
    for i in range(n_chunks):
        inter, S_next = {}, {}
        for b, d in seqs:
            blk = pre[b, d, order[d][i]]
            for h in heads:
                inter[b, d, h] = _dot_nt(blk["q_full"][h], S[b, d, h])
                S_next[b, d, h] = S[b, d, h] * blk["e_end"][h] + _dot_tn(blk["v"][h], blk["k_hat"][h])
        for b, d in seqs:
            c = order[d][i]
            io[d][4][b, c * C:(c + 1) * C, :] = jnp.concatenate(
                [intra[b, d, c, h] + inter[b, d, h] for h in heads], axis=1).astype(BF16)
        S = S_next
    for key, s_new in S.items():
        s_ref[key] = s_new


def _hgrn_intra(d, q, k, v, lf, G):
    C, SUB = HGRN_CHUNK, HGRN_SUB
    Dh = HGRN_HEAD
    n_sub = C // SUB
    _, incl = _order_masks(d, C)
    g_end = jnp.sum(lf, axis=0, keepdims=True)
    W = q.shape[-1]
    anchors = [G[I * SUB + SUB // 2:I * SUB + SUB // 2 + 1, :] for I in range(n_sub)]
    g_anchor = jnp.concatenate([jnp.broadcast_to(a, (SUB, W)) for a in anchors], axis=0)
    q_a = q * jnp.exp(G - g_anchor)
    q_full = q * jnp.exp(G)
    k_hat = k * jnp.exp(g_end - G)
    sub_of_row = lax.broadcasted_iota(jnp.int32, (C, 1), 0) // SUB
    k_anch = []
    for I in range(n_sub):
        visible = (I - sub_of_row) * (1 - 2 * d) >= 0
        k_anch.append(k * jnp.exp(jnp.where(visible, anchors[I] - G, 0.0)))
    e_end = jnp.exp(g_end)
    out = dict(A=[], v=[], q_full=[], k_hat=[], e_end=[])
    for h in range(W // Dh):
        sl = slice(h * Dh, (h + 1) * Dh)
        A = jnp.concatenate([_dot_nt(q_a[I * SUB:(I + 1) * SUB, sl], k_anch[I][:, sl]) for I in range(n_sub)],
                            axis=0)
        out["A"].append(jnp.where(incl, A, 0.0))
        out["v"].append(v[:, sl])
        out["q_full"].append(q_full[:, sl])
        out["k_hat"].append(k_hat[:, sl])
        out["e_end"].append(e_end[:, sl])
    return out


def _hgrn_call(q, v, v_col, k, lf, n_ctx):
    B, N, W = q.shape
    T = HGRN_STEP
    shared, per_dir = _scan_specs(T, W, n_ctx, N, B)
    specs = [s for d in range(2) for s in (shared(d), shared(d, v_col), per_dir(d), per_dir(d))]
    return _scan_call(_hgrn_kernel, "hgrn2_scan", T, specs, (q, v, k, lf) * 2,
                      (2, W // HGRN_HEAD, HGRN_HEAD, HGRN_HEAD), n_ctx)


def _rope_tables(n, head):
    half = head // 2
    inv_freq = ROPE_BASE ** (-jnp.arange(half, dtype=F32) / half)
    ang = jnp.arange(n).astype(F32)[:, None] * inv_freq[None, :]
    cos, sin = jnp.cos(ang), jnp.sin(ang)
    return jnp.concatenate([cos, cos], axis=1), jnp.concatenate([-sin, sin], axis=1)


def _block_diag_ones(width, head):
    idx = jnp.arange(width) // head
    return (idx[:, None] == idx[None, :]).astype(BF16)


def _permute_w_in(w_in_l, v_down_l):
    D = w_in_l.shape[0]
    W = BRANCH_W
    n_rw, n_ret, n_hg = 3 * W + LORA_W, 4 * W, 5 * W
    rw = w_in_l[:, :n_rw]
    ret = w_in_l[:, n_rw:n_rw + n_ret]
    hg = w_in_l[:, n_rw + n_ret:n_rw + n_ret + n_hg]
    gate = w_in_l[:, n_rw + n_ret + n_hg:]
    extra = jnp.zeros((D, VDOWN_PAD), F32)
    if v_down_l is not None:
        extra = extra.at[:, :v_down_l.shape[1]].set(v_down_l)
    main = jnp.concatenate([gate, rw[:, :3 * W], ret, hg[:, :W], hg[:, 3 * W:]], axis=1)
    decay = jnp.concatenate([rw[:, 3 * W:], extra, hg[:, W:3 * W]], axis=1)
    assert main.shape[1] == MAIN_WIDTH and decay.shape[1] == DECAY_WIDTH
    return main.astype(BF16), decay.astype(BF16)


def kernel(x, c, ctx, c_ctx, ada_w, ada_b, norm1_w, norm2_w, w_in, rwkv_mu, rwkv_w0, rwkv_w_up, rwkv_a0, rwkv_a_up, rwkv_g_up, rwkv_k_k, rwkv_k_a, rwkv_r_k, rwkv_lnx_w, rwkv_lnx_b, rwkv_v0, rwkv_v_down, rwkv_v_up, ret_decay, hgrn_lb, hgrn_norm_w, w_branch, w_out, mlp_w1, mlp_w2, final_norm_w):
    B, SEQ, D = x.shape
    n_ctx = ctx.shape[1]
    N = n_ctx + SEQ
    L = ada_w.shape[0]
    W = BRANCH_W
    assert n_ctx == ROW_TILE and SEQ % ROW_TILE == 0 and D == D_MODEL
    assert MAIN_COLS["rwkv"] % (3 * W) == 0 and MAIN_COLS["ret"] % W == 0

    xa = jnp.concatenate([ctx, x], axis=1)
    cvec = jnp.concatenate([c, c_ctx[None], jnp.zeros((8 - B - 1, D), F32)], axis=0)
    mod = _ada_call(cvec, ada_w, ada_b)
    sm = jax.nn.softmax(hgrn_lb.astype(F32), axis=0)
    hgrn_lower = jnp.cumsum(sm, axis=0) - sm[0:1]
    tables = _rope_tables(N, RET_HEAD)
    ones64 = _block_diag_ones(W, RWKV_HEAD)
    ones128 = _block_diag_ones(W, RET_HEAD)
    row = lambda t: t.reshape(1, -1)
    v_first = None
    out = None

    for l in range(L):
        last = l == L - 1
        mod_c = jnp.broadcast_to(mod[l, B].reshape(1, 6, D), (B, 6, D))
        modsel = jnp.concatenate([mod_c, mod[l, :B].reshape(B, 6, D)], axis=1)
        w_main, w_decay = _permute_w_in(w_in[l], None if l == 0 else rwkv_v_down[l - 1])
        P = _proj_call(xa, modsel, norm1_w[l], w_main, n_ctx, BF16, 4)
        Pd = _proj_call(xa, modsel, norm1_w[l], w_decay, n_ctx, F32, 1)

        has_vres = l > 0
        if has_vres:
            v0 = row(rwkv_v0[l - 1])
            v_up = jnp.zeros((VDOWN_PAD, W), F32).at[:rwkv_v_up.shape[1]].set(rwkv_v_up[l - 1]).astype(BF16)
        else:
            v0 = jnp.zeros((1, W), F32)
            v_up = jnp.zeros((VDOWN_PAD, W), BF16)
        prep_w = (row(rwkv_mu[l, :3 * W]), row(rwkv_mu[l, 3 * W:]),
                  rwkv_w_up[l].astype(BF16), rwkv_w0[l], rwkv_a_up[l].astype(BF16), rwkv_a0[l],
                  rwkv_g_up[l].astype(BF16), row(rwkv_k_k[l]), row(rwkv_k_a[l]), row(rwkv_r_k[l]),
                  v0, v_up, ones64, hgrn_lower[l])
        (r, v, kk, lw, kd, a, g, bonus, rq, rk, hq, hk, hlf) = _prep_call(
            P, Pd, P if v_first is None else v_first, tables, prep_w, n_ctx, has_vres)
        if v_first is None:
            v_first = v

        o_rw_dirs = _rwkv_call(r, v, kk, lw, kd, a, n_ctx)
        log_gamma = -jnp.exp(ret_decay[l].astype(F32))
        lg = jnp.repeat(log_gamma, RET_HEAD, axis=-1).reshape(2, 1, W)
        o_ret_dirs = _ret_call(rq, rk, P, (MAIN_COLS["ret"] + 2 * W) // W, lg, n_ctx)
        o_hg_dirs = _hgrn_call(hq, P, MAIN_COLS["hgrn_i"] // W, hk, hlf, n_ctx)

        merge_w = (row(rwkv_lnx_w[l]), row(rwkv_lnx_b[l]), row(hgrn_norm_w[l]), ones64, ones128,
                   w_branch[l].astype(BF16), w_out[l].astype(BF16))
        xm = _merge_call(xa, P, o_rw_dirs, g, bonus, o_ret_dirs, o_hg_dirs, merge_w, modsel, n_ctx, last)
        xa = _mlp_call(xm, modsel, norm2_w[l], mlp_w1[l].astype(BF16), mlp_w2[l].astype(BF16),
                       final_norm_w, n_ctx, (n_ctx // ROW_TILE) if last else 0, last)
        out = xa
    return out
```

```python
import functools

import jax
import jax.numpy as jnp
import numpy as np
from jax import lax
from jax.experimental import pallas as pl
from jax.experimental.pallas import tpu as pltpu

F32 = jnp.float32
BF16 = jnp.bfloat16

NORM_EPS = 1e-6
RWKV_LN_EPS = 64e-5
DECAY_SCALE = 0.6065306597126334
L2_EPS = 1e-12
ROPE_BASE = 10000.0
GRID_W = 64

BRANCH_W = 512
RWKV_HEAD = 64
RET_HEAD = 128
HGRN_HEAD = 128
DECAY_LORA = 64
AAA_LORA = 64
GATE_LORA = 128
LORA_W = 2 * DECAY_LORA + 2 * AAA_LORA + GATE_LORA
VDOWN_PAD = 128

RWKV_CHUNK = 64
RWKV_STEP = 128
HGRN_CHUNK = 64
HGRN_SUB = 32
HGRN_STEP = 256
RET_CHUNK = 128
ROW_TILE = 256

VMEM_LIMIT = 56 * 1024 * 1024

D_MODEL = 2 * BRANCH_W
MAIN_COLS = dict(gate=0, rwkv=3 * D_MODEL, ret=3 * D_MODEL + 3 * BRANCH_W, hgrn_q=3 * D_MODEL + 7 * BRANCH_W,
                 hgrn_i=3 * D_MODEL + 8 * BRANCH_W, hgrn_g=3 * D_MODEL + 9 * BRANCH_W)
MAIN_WIDTH = 3 * D_MODEL + 10 * BRANCH_W
DECAY_COLS = dict(lora=0, v_down=LORA_W, hgrn_f=LORA_W + VDOWN_PAD)
DECAY_WIDTH = LORA_W + VDOWN_PAD + 2 * BRANCH_W


def _sigmoid(x):
    return 1.0 / (1.0 + jnp.exp(-x))


def _dot(a, b):
    return jnp.dot(a.astype(BF16), b.astype(BF16), preferred_element_type=F32)


def _dot_nt(a, b):
    return lax.dot_general(a.astype(BF16), b.astype(BF16), (((1,), (1,)), ((), ())),
                           preferred_element_type=F32)


def _dot_tn(a, b):
    return lax.dot_general(a.astype(BF16), b.astype(BF16), (((0,), (0,)), ((), ())),
                           preferred_element_type=F32)


def _cumsum_rows(mask_f32, x):
    m = mask_f32.astype(BF16)
    hi = x.astype(BF16)
    lo = (x - hi.astype(F32)).astype(BF16)
    return jnp.dot(m, hi, preferred_element_type=F32) + jnp.dot(m, lo, preferred_element_type=F32)


def _head_sum(x, ones_bd, split=True):
    hi = x.astype(BF16)
    out = jnp.dot(hi, ones_bd, preferred_element_type=F32)
    if split:
        lo = (x - hi.astype(F32)).astype(BF16)
        out = out + jnp.dot(lo, ones_bd, preferred_element_type=F32)
    return out


def _ada_kernel(c_ref, w_ref, b_ref, o_ref):
    cv = c_ref[...]
    s = cv * _sigmoid(cv)
    o_ref[0] = _dot(s, w_ref[0]) + b_ref[0]


def _ada_call(cvec, ada_w, ada_b):
    L, D, D6 = ada_w.shape
    tn = D6 // 4
    return pl.pallas_call(
        _ada_kernel,
        grid=(L, D6 // tn),
        in_specs=[pl.BlockSpec((8, D), lambda l, j: (0, 0)),
                  pl.BlockSpec((1, D, tn), lambda l, j: (l, 0, j)),
                  pl.BlockSpec((1, 1, tn), lambda l, j: (l, 0, j))],
        out_specs=pl.BlockSpec((1, 8, tn), lambda l, j: (l, 0, j)),
        out_shape=jax.ShapeDtypeStruct((L, 8, D6), F32),
        compiler_params=pltpu.CompilerParams(vmem_limit_bytes=VMEM_LIMIT),
        name="ada_mod",
    )(cvec, ada_w, ada_b.reshape(L, 1, D6))


def _modulated_norm(xf, nw, mod_ref, is_ctx, which):
    y = xf * lax.rsqrt(jnp.mean(xf * xf, axis=-1, keepdims=True) + NORM_EPS) * nw
    sh = jnp.where(is_ctx, mod_ref[0, which:which + 1, :], mod_ref[0, 6 + which:7 + which, :])
    sc = jnp.where(is_ctx, mod_ref[0, which + 1:which + 2, :], mod_ref[0, 7 + which:8 + which, :])
    return y * (1.0 + sc) + sh


def _is_ctx_rows(tile_idx, tm, n_ctx):
    row = tile_idx * tm + lax.broadcasted_iota(jnp.int32, (tm, 1), 0)
    return row < n_ctx


def _proj_kernel(x_ref, mod_ref, nw_ref, w_ref, o_ref, h_ref, *, tm, n_ctx):
    i = pl.program_id(1)

    @pl.when(pl.program_id(2) == 0)
    def _():
        h = _modulated_norm(x_ref[0], nw_ref[...], mod_ref, _is_ctx_rows(i, tm, n_ctx), 0)
        h_ref[...] = h.astype(BF16)

    o_ref[0] = jnp.dot(h_ref[...], w_ref[...], preferred_element_type=F32).astype(o_ref.dtype)


def _proj_call(xa, modsel, nw, w, n_ctx, out_dtype, n_tiles, tm=768):
    B, N, D = xa.shape
    NP = w.shape[1]
    tn = NP // n_tiles
    return pl.pallas_call(
        functools.partial(_proj_kernel, tm=tm, n_ctx=n_ctx),
        grid=(B, N // tm, n_tiles),
        in_specs=[pl.BlockSpec((1, tm, D), lambda b, i, j: (b, i, 0)),
                  pl.BlockSpec((1, 12, D), lambda b, i, j: (b, 0, 0)),
                  pl.BlockSpec((1, D), lambda b, i, j: (0, 0)),
                  pl.BlockSpec((D, tn), lambda b, i, j: (0, j))],
        out_specs=pl.BlockSpec((1, tm, tn), lambda b, i, j: (b, i, j)),
        out_shape=jax.ShapeDtypeStruct((B, N, NP), out_dtype),
        scratch_shapes=[pltpu.VMEM((tm, D), BF16)],
        compiler_params=pltpu.CompilerParams(
            dimension_semantics=("parallel", "parallel", "arbitrary"),
            vmem_limit_bytes=VMEM_LIMIT),
        name="in_proj",
    )(xa, modsel, nw.reshape(1, D), w)


def _shift_matrices(tm, seq):
    mats = np.zeros((5, tm, tm + 2 * GRID_W), np.float32)
    t = np.arange(tm)
    centre = t + GRID_W
    mats[0, t[1:], centre[1:] - 1] = 0.5
    mats[0, t[:-1], centre[:-1] + 1] = 0.5
    for kind, (first, last) in enumerate(((True, False), (False, False), (False, True), (True, True)), start=1):
        has_left = t % GRID_W != 0
        has_right = t % GRID_W != GRID_W - 1
        has_up = ~(first & (t < GRID_W))
        has_down = ~(last & (t >= tm - GRID_W))
        mats[kind, t[has_left], centre[has_left] - 1] = 0.25
        mats[kind, t[has_right], centre[has_right] + 1] = 0.25
        mats[kind, t[has_up], centre[has_up] - GRID_W] = 0.25
        mats[kind, t[has_down], centre[has_down] + GRID_W] = 0.25
    n_lat = seq // tm

    def kind_of_tile(i):
        latent = jnp.where(i == 1, 4 if n_lat == 1 else 1, jnp.where(i == n_lat, 3, 2))
        return jnp.where(i == 0, 0, latent)

    return jnp.asarray(mats, BF16), kind_of_tile


def _shift_mix(cur, prev, nxt, mu, s):
    ext = jnp.concatenate([prev, cur, nxt], axis=0)
    if ext.dtype == BF16:
        shifted = jnp.dot(s, ext, preferred_element_type=F32)
    else:
        hi = ext.astype(BF16)
        lo = (ext - hi.astype(F32)).astype(BF16)
        shifted = jnp.dot(s, hi, preferred_element_type=F32) + jnp.dot(s, lo, preferred_element_type=F32)
    cur = cur.astype(F32)
    return cur + mu * (shifted - cur)


def _prep_kernel(pc_ref, pp_ref, pn_ref, lc_ref, lp_ref, ln_ref, hv_ref, vf_ref, rq_ref, rk_ref, cs_ref, sn_ref,
                 hq_ref, hf0_ref, hf1_ref,
                 mum_ref, mul_ref, wup_ref, w0_ref, aup_ref, a0_ref, gup_ref, kk_w_ref, ka_ref, r_k_ref,
                 v0_ref, vup_ref, ones_ref, lb_ref,
                 shift_ref,
                 r_o, v_o, kk_o, lw_o, kd_o, a_o, g_o, bonus_o, rq_o, rk_o, hq_o, hk_o, hlf_o,
                 *, has_vres):
    W = BRANCH_W

    p = _shift_mix(pc_ref[0], pp_ref[0], pn_ref[0], mum_ref[...], shift_ref[0])
    lo = _shift_mix(lc_ref[0], lp_ref[0], ln_ref[0], mul_ref[...], shift_ref[0])
    r, k, v = p[:, :W], p[:, W:2 * W], p[:, 2 * W:]
    if has_vres:
        mix = _sigmoid(v0_ref[...] + _dot(hv_ref[0], vup_ref[...]))
        v = v + (vf_ref[0].astype(F32) - v) * mix
    r_o[0] = r.astype(r_o.dtype)
    v_o[0] = v.astype(v_o.dtype)
    gd = lo[:, 2 * DECAY_LORA + 2 * AAA_LORA:]
    g_o[0] = _dot(_sigmoid(gd), gup_ref[...]).astype(g_o.dtype)
    kx = k * kk_w_ref[...]
    kk_o[0] = (kx * lax.rsqrt(_head_sum(kx * kx, ones_ref[...], split=False) + L2_EPS)).astype(kk_o.dtype)
    k_sum = None
    for d in range(2):
        wd = lo[:, d * DECAY_LORA:(d + 1) * DECAY_LORA]
        ad = lo[:, 2 * DECAY_LORA + d * AAA_LORA:2 * DECAY_LORA + (d + 1) * AAA_LORA]
        z = -(w0_ref[d:d + 1, :] + _dot(jnp.tanh(wd), wup_ref[d]))
        lw_o[d, 0] = -DECAY_SCALE / (1.0 + jnp.exp(z))
        a = _sigmoid(a0_ref[d:d + 1, :] + _dot(ad, aup_ref[d]))
        a_o[d, 0] = a.astype(a_o.dtype)
        kd = k * (1.0 + (a - 1.0) * ka_ref[...])
        kd_o[d, 0] = kd.astype(kd_o.dtype)
        k_sum = kd if k_sum is None else k_sum + kd
    bonus_o[0] = (_head_sum(r * k_sum * r_k_ref[...], ones_ref[...], split=False) * v).astype(bonus_o.dtype)

    cosf = jnp.concatenate([cs_ref[...]] * (W // RET_HEAD), axis=1)
    sinf = jnp.concatenate([sn_ref[...]] * (W // RET_HEAD), axis=1)
    lane = lax.broadcasted_iota(jnp.int32, (1, W), 1)
    first_half = (lane & (RET_HEAD - 1)) < RET_HEAD // 2

    def rope(t):
        rot = jnp.where(first_half, pltpu.roll(t, W - RET_HEAD // 2, 1), pltpu.roll(t, RET_HEAD // 2, 1))
        return t * cosf + rot * sinf

    rq_o[0] = rope(rq_ref[0].astype(F32)).astype(rq_o.dtype)
    rk_o[0] = (rope(rk_ref[0].astype(F32)) * (RET_HEAD ** -0.5)).astype(rk_o.dtype)

    q = hq_ref[0].astype(F32)
    hq_o[0] = (q * _sigmoid(q)).astype(hq_o.dtype)
    for d, f_ref in enumerate((hf0_ref, hf1_ref)):
        f = f_ref[0]
        e = jnp.exp(-jnp.abs(f))
        inv = 1.0 / (1.0 + e)
        sig_pos = jnp.where(f >= 0, inv, e * inv)
        sig_neg = jnp.where(f >= 0, e * inv, inv)
        lb = lb_ref[d:d + 1, :]
        hlf_o[d, 0] = jnp.log(lb + (1.0 - lb) * sig_pos)
        hk_o[d, 0] = ((1.0 - lb) * sig_neg).astype(hk_o.dtype)


def _prep_call(P, Pd, v_first, tables, wts, n_ctx, has_vres):
    B, N, _ = P.shape
    W = BRANCH_W
    tm = ROW_TILE
    seq = N - n_ctx
    halo_per_tile = tm // GRID_W
    n_halo = N // GRID_W
    c_main = MAIN_COLS["rwkv"] // (3 * W)
    c_rq = MAIN_COLS["ret"] // W
    c_hq = MAIN_COLS["hgrn_q"] // W
    c_lora = DECAY_COLS["lora"] // LORA_W
    c_vd = DECAY_COLS["v_down"] // VDOWN_PAD
    c_hf = DECAY_COLS["hgrn_f"] // W

    def cur(width, col):
        return pl.BlockSpec((1, tm, width), lambda b, i: (b, i, col))

    def prev(width, col):
        return pl.BlockSpec((1, GRID_W, width), lambda b, i: (b, jnp.maximum(i * halo_per_tile - 1, 0), col))

    def nxt(width, col):
        return pl.BlockSpec((1, GRID_W, width),
                            lambda b, i: (b, jnp.minimum((i + 1) * halo_per_tile, n_halo - 1), col))

    def const(shape):
        return pl.BlockSpec(shape, lambda b, i: (0,) * len(shape))

    tok = pl.BlockSpec((1, tm, W), lambda b, i: (b, i, 0))
    tok2 = pl.BlockSpec((2, 1, tm, W), lambda b, i: (0, b, i, 0))
    cos, sin = tables
    in_specs = [cur(3 * W, c_main), prev(3 * W, c_main), nxt(3 * W, c_main),
                cur(LORA_W, c_lora), prev(LORA_W, c_lora), nxt(LORA_W, c_lora),
                cur(VDOWN_PAD, c_vd), tok, cur(W, c_rq), cur(W, c_rq + 1),
                pl.BlockSpec((tm, RET_HEAD), lambda b, i: (i, 0)),
                pl.BlockSpec((tm, RET_HEAD), lambda b, i: (i, 0)),
                cur(W, c_hq), cur(W, c_hf), cur(W, c_hf + 1)]
    in_specs += [const(w.shape) for w in wts]
    shift_mats, kind_of_tile = _shift_matrices(tm, seq)
    in_specs.append(pl.BlockSpec((1,) + shift_mats.shape[1:], lambda b, i: (kind_of_tile(i), 0, 0)))
    one = jax.ShapeDtypeStruct((B, N, W), BF16)
    two = jax.ShapeDtypeStruct((2, B, N, W), BF16)
    two_f32 = jax.ShapeDtypeStruct((2, B, N, W), F32)
    out_shape = [one, one, one, two_f32, two, two, one, one, one, one, one, two, two_f32]
    out_specs = [tok, tok, tok, tok2, tok2, tok2, tok, tok, tok, tok, tok, tok2, tok2]
    return pl.pallas_call(
        functools.partial(_prep_kernel, has_vres=has_vres),
        grid=(B, N // tm),
        in_specs=in_specs,
        out_specs=out_specs,
        out_shape=out_shape,
        compiler_params=pltpu.CompilerParams(
            dimension_semantics=("parallel", "parallel"), vmem_limit_bytes=VMEM_LIMIT),
        name="branch_prep",
    )(P, P, P, Pd, Pd, Pd, Pd, v_first, P, P, cos, sin, P, Pd, Pd, *wts, shift_mats)


def _merge_kernel(x_ref, g0_ref, g1_ref, g2_ref, orwf_ref, orwb_ref, g_ref, bonus_ref, oretf_ref, oretb_ref,
                  rg_ref, ohgf_ref, ohgb_ref, hg_ref,
                  lnw_ref, lnb_ref, hnw_ref, m64_ref, m128_ref, wb_ref, wo_ref, mod_ref, o_ref,
                  *, tm, n_ctx, tile_off):
    i = pl.program_id(1) + tile_off

    o_rw = orwf_ref[0].astype(F32) + orwb_ref[0].astype(F32)
    o_ret = oretf_ref[0].astype(F32) + oretb_ref[0].astype(F32)
    o_hg = ohgf_ref[0].astype(F32) + ohgb_ref[0].astype(F32)
    sum_rw = _head_sum(o_rw, m64_ref[...])
    ms_ret = _head_sum(o_ret * o_ret, m128_ref[...], split=False) * (1.0 / RET_HEAD)
    ms_hg = _head_sum(o_hg * o_hg, m128_ref[...], split=False) * (1.0 / HGRN_HEAD)
    cen = o_rw - sum_rw * (1.0 / RWKV_HEAD)
    var = _head_sum(cen * cen, m64_ref[...], split=False) * (1.0 / RWKV_HEAD)
    rg = rg_ref[0].astype(F32)
    y_ret = o_ret * lax.rsqrt(ms_ret + NORM_EPS) * (rg * _sigmoid(rg))
    hg = hg_ref[0].astype(F32)
    y_hg = o_hg * lax.rsqrt(ms_hg + NORM_EPS) * hnw_ref[...] * (hg * _sigmoid(hg))
    y_rw = ((cen * lax.rsqrt(var + RWKV_LN_EPS) * lnw_ref[...] + lnb_ref[...] + bonus_ref[0].astype(F32))
            * g_ref[0].astype(F32))

    merged = None
    for b, y, gate_ref in ((1, y_ret, g1_ref), (2, y_hg, g2_ref), (0, y_rw, g0_ref)):
        z = _sigmoid(gate_ref[0].astype(F32)) * _dot(y, wb_ref[b])
        merged = z if merged is None else merged + z
    out = _dot(merged, wo_ref[...])
    g1 = jnp.where(_is_ctx_rows(i, tm, n_ctx), mod_ref[0, 2:3, :], mod_ref[0, 8:9, :])
    o_ref[0] = x_ref[0] + g1 * out


def _merge_call(xa, P, o_rw, g, bonus, o_ret, o_hg, wts, modsel, n_ctx, latent_only):
    B, N, D = xa.shape
    W = BRANCH_W
    tm = ROW_TILE
    off = n_ctx // tm if latent_only else 0
    n_rows = N - off * tm
    c_gate = MAIN_COLS["gate"] // D
    c_rg = (MAIN_COLS["ret"] + 3 * W) // W
    c_hg = MAIN_COLS["hgrn_g"] // W

    def col(width, c):
        return pl.BlockSpec((1, tm, width), lambda b, i: (b, i + off, c))

    def const(shape):
        return pl.BlockSpec(shape, lambda b, i: (0,) * len(shape))

    tok = col(W, 0)
    return pl.pallas_call(
        functools.partial(_merge_kernel, tm=tm, n_ctx=n_ctx, tile_off=off),
        grid=(B, n_rows // tm),
        in_specs=[col(D, 0), col(D, c_gate), col(D, c_gate + 1), col(D, c_gate + 2),
                  tok, tok, tok, tok, tok, tok, col(W, c_rg), tok, tok, col(W, c_hg)]
                 + [const(w.shape) for w in wts]
                 + [pl.BlockSpec((1, 12, D), lambda b, i: (b, 0, 0))],
        out_specs=pl.BlockSpec((1, tm, D), lambda b, i: (b, i, 0)),
        out_shape=jax.ShapeDtypeStruct((B, n_rows, D), F32),
        compiler_params=pltpu.CompilerParams(
            dimension_semantics=("parallel", "parallel"), vmem_limit_bytes=VMEM_LIMIT),
        name="merge_out",
    )(xa, P, P, P, o_rw[0], o_rw[1], g, bonus, o_ret[0], o_ret[1], P, o_hg[0], o_hg[1], P, *wts, modsel)


def _mlp_kernel(x_ref, mod_ref, nw_ref, w1_ref, w2_ref, fw_ref, o_ref, *, tm, n_ctx, tile_off, final):
    i = pl.program_id(1) + tile_off
    is_ctx = _is_ctx_rows(i, tm, n_ctx)
    xf = x_ref[0]
    h = _modulated_norm(xf, nw_ref[...], mod_ref, is_ctx, 3)
    a = jnp.maximum(_dot(h, w1_ref[...]), 0.0)
    out = _dot(a * a, w2_ref[...])
    g2 = jnp.where(is_ctx, mod_ref[0, 5:6, :], mod_ref[0, 11:12, :])
    xn = xf + g2 * out
    if final:
        xn = xn * lax.rsqrt(jnp.mean(xn * xn, axis=-1, keepdims=True) + NORM_EPS) * fw_ref[...]
    o_ref[0] = xn


def _mlp_call(xa, modsel, nw, w1, w2, fw, n_ctx, tile_off, final):
    B, n_rows, D = xa.shape
    DF = w1.shape[1]
    tm = ROW_TILE
    return pl.pallas_call(
        functools.partial(_mlp_kernel, tm=tm, n_ctx=n_ctx, tile_off=tile_off, final=final),
        grid=(B, n_rows // tm),
        in_specs=[pl.BlockSpec((1, tm, D), lambda b, i: (b, i, 0)),
                  pl.BlockSpec((1, 12, D), lambda b, i: (b, 0, 0)),
                  pl.BlockSpec((1, D), lambda b, i: (0, 0)),
                  pl.BlockSpec((D, DF), lambda b, i: (0, 0)),
                  pl.BlockSpec((DF, D), lambda b, i: (0, 0)),
                  pl.BlockSpec((1, D), lambda b, i: (0, 0))],
        out_specs=pl.BlockSpec((1, tm, D), lambda b, i: (b, i, 0)),
        out_shape=jax.ShapeDtypeStruct((B, n_rows, D), F32),
        compiler_params=pltpu.CompilerParams(
            dimension_semantics=("parallel", "parallel"), vmem_limit_bytes=VMEM_LIMIT),
        name="mlp",
    )(xa, modsel, nw.reshape(1, D), w1, w2, fw.reshape(1, D))


def _order_masks(d, n):
    t = lax.broadcasted_iota(jnp.int32, (n, n), 0)
    s = lax.broadcasted_iota(jnp.int32, (n, n), 1)
    lead = (t - s) * (1 - 2 * d)
    return lead > 0, lead >= 0


def _scan_specs(T, W, n_ctx, N, B):
    n_ctx_blocks, n_blocks = n_ctx // T, N // T

    def block(d, j):
        if d == 0:
            return j
        return jnp.where(j < n_ctx_blocks, n_ctx_blocks - 1 - j, n_blocks - 1 + n_ctx_blocks - j)

    def shared(d, col=0):
        return pl.BlockSpec((B, T, W), lambda j: (0, block(d, j), col))

    def per_dir(d):
        return pl.BlockSpec((1, B, T, W), lambda j: (d, 0, block(d, j), 0))

    return shared, per_dir


def _scan_call(kernel_fn, name, T, specs, args, state_shape, n_ctx):
    B, N, W = args[0].shape[0], args[0].shape[1], BRANCH_W
    shared, _ = _scan_specs(T, W, n_ctx, N, B)
    out = jax.ShapeDtypeStruct((B, N, W), BF16)
    return pl.pallas_call(
        kernel_fn,
        grid=(N // T,),
        in_specs=specs,
        out_specs=[shared(0), shared(1)],
        out_shape=[out, out],
        scratch_shapes=[pltpu.VMEM((B,) + state_shape, F32)],
        compiler_params=pltpu.CompilerParams(
            dimension_semantics=("arbitrary",), vmem_limit_bytes=VMEM_LIMIT),
        name=name,
    )(*args)


def _load(ref, b, rows=slice(None)):
    x = ref[b, rows, :] if len(ref.shape) == 3 else ref[0, b, rows, :]
    return x.astype(F32)


def _rwkv_kernel(*refs):
    sides, (of_ref, ob_ref, s_ref) = (refs[0:6], refs[6:12]), refs[12:]
    n_batch, _, n_groups = s_ref.shape[:3]

    @pl.when(pl.program_id(0) == 0)
    def _():
        s_ref[...] = jnp.zeros_like(s_ref)

    C = RWKV_CHUNK
    n_sub = RWKV_STEP // C
    def rows(d, i):
        c = i if d == 0 else n_sub - 1 - i
        return slice(c * C, (c + 1) * C)

    seqs = [(b, d, i) for b in range(n_batch) for d in range(2) for i in range(n_sub)]
    ins = {(b, d, i): tuple(_load(ref, b, rows(d, i)) for ref in sides[d]) for b, d, i in seqs}
    pre = _rwkv_pre(seqs, ins, n_groups)
    S = {(b, d, g): s_ref[b, d, g] for b in range(n_batch) for d in range(2) for g in range(n_groups)}
    for i in range(n_sub):
        now = [sq for sq in seqs if sq[2] == i]
        outs, S = _rwkv_apply(now, pre, S, n_groups)
        for b, d, _ in now:
            (of_ref, ob_ref)[d][b, rows(d, i), :] = outs[b, d, i].astype(BF16)
    for key, s_new in S.items():
        s_ref[key] = s_new


def _rwkv_pre(seqs, ins, n_groups):
    C = RWKV_CHUNK
    HG = 4 * RWKV_HEAD
    chains = [sq + (g,) for sq in seqs for g in range(n_groups)]

    rb = lax.broadcasted_iota(jnp.int32, (4 * C, HG), 0) // C
    lb = lax.broadcasted_iota(jnp.int32, (4 * C, HG), 1) // RWKV_HEAD
    same_head = rb == lb
    head_mask = same_head.astype(F32).astype(BF16)
    t_idx = lax.broadcasted_iota(jnp.int32, (C, HG), 0)
    s_idx = lax.broadcasted_iota(jnp.int32, (C, HG), 1) % C

    def bd(x):
        xb = x.astype(BF16)
        return jnp.concatenate([xb, xb, xb, xb], axis=0) * head_mask

    incl_c = [_order_masks(d, C)[1].astype(F32) for d in range(2)]
    G = {sq: _cumsum_rows(incl_c[sq[1]], ins[sq][3]) for sq in seqs}
    strict, incl, eye = {}, {}, {}
    for d in range(2):
        lead = (t_idx - s_idx) * (1 - 2 * d)
        strict[d], incl[d], eye[d] = lead > 0, lead >= 0, (lead == 0).astype(F32)
    q_t, r_t, p_t, k_t, p_hat, k_hat, v_g, e_end = {}, {}, {}, {}, {}, {}, {}, {}
    for sq in seqs:
        r, v, kk, lw, k, a = ins[sq]
        g_end = jnp.sum(lw, axis=0, keepdims=True)
        e_neg = jnp.exp(-G[sq])
        end = jnp.exp(g_end)
        full = dict(q=kk * jnp.exp(G[sq] - lw), r=r * jnp.exp(G[sq]), p=-(a * kk) * e_neg, k=k * e_neg, v=v)
        for g in range(n_groups):
            sl = slice(g * HG, (g + 1) * HG)
            c = sq + (g,)
            q_t[c], r_t[c], p_t[c], k_t[c], v_g[c] = (full[n][:, sl] for n in "qrpkv")
            e_end[c] = end[:, sl]
            p_hat[c], k_hat[c] = p_t[c] * e_end[c], k_t[c] * e_end[c]

    bv = {c: bd(v_g[c]) for c in chains}
    qr = {c: jnp.concatenate([q_t[c], r_t[c]], axis=0).astype(BF16) for c in chains}
    a_p = {c: _dot_nt(qr[c], bd(p_t[c])) for c in chains}
    a_k = {c: _dot_nt(qr[c], bd(k_t[c])) for c in chains}
    n_qp = {c: jnp.where(strict[c[1]], a_p[c][:C], 0.0) for c in chains}
    a_qk = {c: jnp.where(strict[c[1]], a_k[c][:C], 0.0) for c in chains}
    a_rp = {c: jnp.where(incl[c[1]], a_p[c][C:], 0.0) for c in chains}
    a_rk = {c: jnp.where(incl[c[1]], a_k[c][C:], 0.0) for c in chains}

    R = {c: _dot(n_qp[c], bd(n_qp[c])) for c in chains}
    T = {c: eye[c[1]] + n_qp[c] for c in chains}
    avk = {c: _dot(jnp.concatenate([a_qk[c], a_rk[c]], axis=0), bv[c]) for c in chains}
    span = 4
    while span < C:
        RT = {c: _dot(jnp.concatenate([R[c], T[c]], axis=0), bd(R[c])) for c in chains}
        R = {c: RT[c][:C] for c in chains}
        T = {c: T[c] + RT[c][C:] for c in chains}
        span *= 2
    TR = {c: _dot(T[c], bd(R[c])) for c in chains}
    Tb = {c: (T[c] + TR[c]).astype(BF16) for c in chains}

    w = {c: _dot(Tb[c], bd(q_t[c])) for c in chains}
    u0 = {c: _dot(Tb[c], bd(avk[c][:C])) for c in chains}
    wr = {c: jnp.concatenate([w[c], r_t[c]], axis=0).astype(BF16) for c in chains}
    pk_hat = {c: jnp.concatenate([p_hat[c], k_hat[c]], axis=0).astype(BF16) for c in chains}
    return dict(bd=bd, same_head=same_head, wr=wr, u0=u0, a_rp=a_rp, o_v={c: avk[c][C:] for c in chains},
                pk_hat=pk_hat, v=v_g, e_end=e_end)


def _rwkv_apply(seqs, pre, S, n_groups):
    C = RWKV_CHUNK
    bd, same_head = pre["bd"], pre["same_head"]
    chains = [sq + (g,) for sq in seqs for g in range(n_groups)]
    state_of = {c: (c[0], c[1], c[3]) for c in chains}
    Sb = {c: S[state_of[c]].astype(BF16) for c in chains}
    ws = {c: _dot_nt(pre["wr"][c], Sb[c]) for c in chains}
    u = {c: ws[c][:C] + pre["u0"][c] for c in chains}
    o = {c: _dot(pre["a_rp"][c], bd(u[c])) + pre["o_v"][c] + ws[c][C:] for c in chains}
    upd = {c: _dot_tn(jnp.concatenate([u[c], pre["v"][c]], axis=0), pre["pk_hat"][c]) for c in chains}
    S_new = dict(S)
    for c in chains:
        S_new[state_of[c]] = S[state_of[c]] * pre["e_end"][c] + jnp.where(same_head, upd[c], 0.0)
    outs = {sq: jnp.concatenate([o[sq + (g,)] for g in range(n_groups)], axis=1) for sq in seqs}
    return outs, S_new


def _rwkv_call(r, v, kk, lw, k, a, n_ctx):
    B, N, W = r.shape
    T = RWKV_STEP
    assert RWKV_CHUNK == RWKV_HEAD and W % (4 * RWKV_HEAD) == 0
    shared, per_dir = _scan_specs(T, W, n_ctx, N, B)
    specs = [s for d in range(2) for s in (shared(d), shared(d), shared(d), per_dir(d), per_dir(d), per_dir(d))]
    args = (r, v, kk, lw, k, a) * 2
    n_groups = W // (4 * RWKV_HEAD)
    return _scan_call(_rwkv_kernel, "rwkv7_scan", T, specs, args,
                      (2, n_groups, 4 * RWKV_HEAD, 4 * RWKV_HEAD), n_ctx)


def _ret_kernel(qf_ref, kf_ref, vf_ref, qb_ref, kb_ref, vb_ref, lg_ref, of_ref, ob_ref, s_ref):
    C = RET_CHUNK
    Dh = RET_HEAD

    @pl.when(pl.program_id(0) == 0)
    def _():
        s_ref[...] = jnp.zeros_like(s_ref)

    t = lax.broadcasted_iota(jnp.int32, (C, C), 0)
    s = lax.broadcasted_iota(jnp.int32, (C, C), 1)
    dist = jnp.abs(t - s).astype(F32)
    tcol = lax.broadcasted_iota(jnp.int32, (C, 1), 0)
    io = ((qf_ref, kf_ref, vf_ref, of_ref), (qb_ref, kb_ref, vb_ref, ob_ref))
    n_batch = s_ref.shape[0]
    n_heads = BRANCH_W // Dh
    chains = [(b, d, h) for b in range(n_batch) for d in range(2) for h in range(n_heads)]
    incl = [_order_masks(d, C)[1] for d in range(2)]
    n_t = [(tcol + 1 if d == 0 else C - tcol).astype(F32) for d in range(2)]

    def head(ref, b, h):
        return ref[b, :, h * Dh:(h + 1) * Dh].astype(F32)

    scores = {(b, d, h): _dot_nt(head(io[d][0], b, h), head(io[d][1], b, h)) for b, d, h in chains}
    outs, H = {}, {}
    for b, d, h in chains:
        lg = lg_ref[d, :, h * Dh:h * Dh + C]
        lgd = lg_ref[d, :, h * Dh:(h + 1) * Dh]
        A = scores[b, d, h] * jnp.where(incl[d], jnp.exp(lg * dist), 0.0)
        H[b, d, h] = s_ref[b, d, h]
        outs[b, d, h] = (_dot(A, head(io[d][2], b, h))
                         + _dot(head(io[d][0], b, h) * jnp.exp(lgd * n_t[d]), H[b, d, h]))
    for b, d, h in chains:
        lgd = lg_ref[d, :, h * Dh:(h + 1) * Dh]
        k_hat = head(io[d][1], b, h) * jnp.exp(lgd * (C - n_t[d]))
        s_ref[b, d, h] = H[b, d, h] * jnp.exp(lgd * C) + _dot_tn(k_hat, head(io[d][2], b, h))
    for b in range(n_batch):
        for d in range(2):
            io[d][3][b] = jnp.concatenate([outs[b, d, h] for h in range(n_heads)], axis=1).astype(BF16)


def _ret_call(q, k, v, v_col, lg, n_ctx):
    B, N, W = q.shape
    C = RET_CHUNK
    assert C <= RET_HEAD
    shared, _ = _scan_specs(C, W, n_ctx, N, B)
    specs = [s for d in range(2) for s in (shared(d), shared(d), shared(d, v_col))]
    specs.append(pl.BlockSpec((2, 1, W), lambda j: (0, 0, 0)))
    return _scan_call(_ret_kernel, "retention_scan", C, specs, (q, k, v, q, k, v, lg),
                      (2, W // RET_HEAD, RET_HEAD, RET_HEAD), n_ctx)


def _hgrn_kernel(qf_ref, vf_ref, kf_ref, lff_ref, qb_ref, vb_ref, kb_ref, lfb_ref, of_ref, ob_ref, s_ref):
    C = HGRN_CHUNK
    n_batch, _, n_heads = s_ref.shape[:3]

    @pl.when(pl.program_id(0) == 0)
    def _():
        s_ref[...] = jnp.zeros_like(s_ref)

    io = ((qf_ref, vf_ref, kf_ref, lff_ref, of_ref), (qb_ref, vb_ref, kb_ref, lfb_ref, ob_ref))
    n_chunks = HGRN_STEP // C
    order = [list(range(n_chunks)), list(reversed(range(n_chunks)))]
    seqs = [(b, d) for b in range(n_batch) for d in range(2)]
    blocks = [sq + (c,) for sq in seqs for c in range(n_chunks)]
    incl_c = [_order_masks(d, C)[1].astype(F32) for d in range(2)]

    def rows(ref, b, c):
        x = ref[b, c * C:(c + 1) * C, :] if len(ref.shape) == 3 else ref[0, b, c * C:(c + 1) * C, :]
        return x.astype(F32)

    G = {(b, d, c): _cumsum_rows(incl_c[d], rows(io[d][3], b, c)) for b, d, c in blocks}
    pre = {}
    for b, d, c in blocks:
        q_ref, v_ref, k_ref, lf_ref, _ = io[d]
        pre[b, d, c] = _hgrn_intra(d, rows(q_ref, b, c), rows(k_ref, b, c), rows(v_ref, b, c),
                                   rows(lf_ref, b, c), G[b, d, c])
    heads = range(n_heads)
    intra = {blk + (h,): _dot(pre[blk]["A"][h], pre[blk]["v"][h]) for blk in blocks for h in heads}

    S = {sq + (h,): s_ref[sq + (h,)] for sq in seqs for h in heads}
    for i in range(n_chunks):
        inter, S_next = {}, {}
        for b, d in seqs:
            blk = pre[b, d, order[d][i]]
            for h in heads:
                inter[b, d, h] = _dot_nt(blk["q_full"][h], S[b, d, h])
                S_next[b, d, h] = S[b, d, h] * blk["e_end"][h] + _dot_tn(blk["v"][h], blk["k_hat"][h])
        for b, d in seqs:
            c = order[d][i]
            io[d][4][b, c * C:(c + 1) * C, :] = jnp.concatenate(
                [intra[b, d, c, h] + inter[b, d, h] for h in heads], axis=1).astype(BF16)
        S = S_next
    for key, s_new in S.items():
        s_ref[key] = s_new


def _hgrn_intra(d, q, k, v, lf, G):
    C, SUB = HGRN_CHUNK, HGRN_SUB
    Dh = HGRN_HEAD
    n_sub = C // SUB
    _, incl = _order_masks(d, C)
    g_end = jnp.sum(lf, axis=0, keepdims=True)
    W = q.shape[-1]
    anchors = [G[I * SUB + SUB // 2:I * SUB + SUB // 2 + 1, :] for I in range(n_sub)]
    g_anchor = jnp.concatenate([jnp.broadcast_to(a, (SUB, W)) for a in anchors], axis=0)
    q_a = q * jnp.exp(G - g_anchor)
    q_full = q * jnp.exp(G)
    k_hat = k * jnp.exp(g_end - G)
    sub_of_row = lax.broadcasted_iota(jnp.int32, (C, 1), 0) // SUB
    k_anch = []
    for I in range(n_sub):
        visible = (I - sub_of_row) * (1 - 2 * d) >= 0
        k_anch.append(k * jnp.exp(jnp.where(visible, anchors[I] - G, 0.0)))
    e_end = jnp.exp(g_end)
    out = dict(A=[], v=[], q_full=[], k_hat=[], e_end=[])
    for h in range(W // Dh):
        sl = slice(h * Dh, (h + 1) * Dh)
        A = jnp.concatenate([_dot_nt(q_a[I * SUB:(I + 1) * SUB, sl], k_anch[I][:, sl]) for I in range(n_sub)],
                            axis=0)
        out["A"].append(jnp.where(incl, A, 0.0))
        out["v"].append(v[:, sl])
        out["q_full"].append(q_full[:, sl])
        out["k_hat"].append(k_hat[:, sl])
        out["e_end"].append(e_end[:, sl])
    return out


def _hgrn_call(q, v, v_col, k, lf, n_ctx):
    B, N, W = q.shape
    T = HGRN_STEP
    shared, per_dir = _scan_specs(T, W, n_ctx, N, B)
    specs = [s for d in range(2) for s in (shared(d), shared(d, v_col), per_dir(d), per_dir(d))]
    return _scan_call(_hgrn_kernel, "hgrn2_scan", T, specs, (q, v, k, lf) * 2,
                      (2, W // HGRN_HEAD, HGRN_HEAD, HGRN_HEAD), n_ctx)


def _rope_tables(n, head):
    half = head // 2
    inv_freq = ROPE_BASE ** (-jnp.arange(half, dtype=F32) / half)
    ang = jnp.arange(n).astype(F32)[:, None] * inv_freq[None, :]
    cos, sin = jnp.cos(ang), jnp.sin(ang)
    return jnp.concatenate([cos, cos], axis=1), jnp.concatenate([-sin, sin], axis=1)


def _block_diag_ones(width, head):
    idx = jnp.arange(width) // head
    return (idx[:, None] == idx[None, :]).astype(BF16)


def _permute_w_in(w_in_l, v_down_l):
    D = w_in_l.shape[0]
    W = BRANCH_W
    n_rw, n_ret, n_hg = 3 * W + LORA_W, 4 * W, 5 * W
    rw = w_in_l[:, :n_rw]
    ret = w_in_l[:, n_rw:n_rw + n_ret]
    hg = w_in_l[:, n_rw + n_ret:n_rw + n_ret + n_hg]
    gate = w_in_l[:, n_rw + n_ret + n_hg:]
    extra = jnp.zeros((D, VDOWN_PAD), F32)
    if v_down_l is not None:
        extra = extra.at[:, :v_down_l.shape[1]].set(v_down_l)
    main = jnp.concatenate([gate, rw[:, :3 * W], ret, hg[:, :W], hg[:, 3 * W:]], axis=1)
    decay = jnp.concatenate([rw[:, 3 * W:], extra, hg[:, W:3 * W]], axis=1)
    assert main.shape[1] == MAIN_WIDTH and decay.shape[1] == DECAY_WIDTH
    return main.astype(BF16), decay.astype(BF16)


def kernel(x, c, ctx, c_ctx, ada_w, ada_b, norm1_w, norm2_w, w_in, rwkv_mu, rwkv_w0, rwkv_w_up, rwkv_a0, rwkv_a_up, rwkv_g_up, rwkv_k_k, rwkv_k_a, rwkv_r_k, rwkv_lnx_w, rwkv_lnx_b, rwkv_v0, rwkv_v_down, rwkv_v_up, ret_decay, hgrn_lb, hgrn_norm_w, w_branch, w_out, mlp_w1, mlp_w2, final_norm_w):
    B, SEQ, D = x.shape
    n_ctx = ctx.shape[1]
    N = n_ctx + SEQ
    L = ada_w.shape[0]
    W = BRANCH_W
    assert n_ctx == ROW_TILE and SEQ % ROW_TILE == 0 and D == D_MODEL
    assert MAIN_COLS["rwkv"] % (3 * W) == 0 and MAIN_COLS["ret"] % W == 0

    xa = jnp.concatenate([ctx, x], axis=1)
    cvec = jnp.concatenate([c, c_ctx[None], jnp.zeros((8 - B - 1, D), F32)], axis=0)
    mod = _ada_call(cvec, ada_w, ada_b)
    sm = jax.nn.softmax(hgrn_lb.astype(F32), axis=0)
    hgrn_lower = jnp.cumsum(sm, axis=0) - sm[0:1]
    tables = _rope_tables(N, RET_HEAD)
    ones64 = _block_diag_ones(W, RWKV_HEAD)
    ones128 = _block_diag_ones(W, RET_HEAD)
    row = lambda t: t.reshape(1, -1)
    v_first = None
    out = None

    for l in range(L):
        last = l == L - 1
        mod_c = jnp.broadcast_to(mod[l, B].reshape(1, 6, D), (B, 6, D))
        modsel = jnp.concatenate([mod_c, mod[l, :B].reshape(B, 6, D)], axis=1)
        w_main, w_decay = _permute_w_in(w_in[l], None if l == 0 else rwkv_v_down[l - 1])
        P = _proj_call(xa, modsel, norm1_w[l], w_main, n_ctx, BF16, 4)
        Pd = _proj_call(xa, modsel, norm1_w[l], w_decay, n_ctx, F32, 1)

        has_vres = l > 0
        if has_vres:
            v0 = row(rwkv_v0[l - 1])
            v_up = jnp.zeros((VDOWN_PAD, W), F32).at[:rwkv_v_up.shape[1]].set(rwkv_v_up[l - 1]).astype(BF16)
        else:
            v0 = jnp.zeros((1, W), F32)
            v_up = jnp.zeros((VDOWN_PAD, W), BF16)
        prep_w = (row(rwkv_mu[l, :3 * W]), row(rwkv_mu[l, 3 * W:]),
                  rwkv_w_up[l].astype(BF16), rwkv_w0[l], rwkv_a_up[l].astype(BF16), rwkv_a0[l],
                  rwkv_g_up[l].astype(BF16), row(rwkv_k_k[l]), row(rwkv_k_a[l]), row(rwkv_r_k[l]),
                  v0, v_up, ones64, hgrn_lower[l])
        (r, v, kk, lw, kd, a, g, bonus, rq, rk, hq, hk, hlf) = _prep_call(
            P, Pd, P if v_first is None else v_first, tables, prep_w, n_ctx, has_vres)
        if v_first is None:
            v_first = v

        o_rw_dirs = _rwkv_call(r, v, kk, lw, kd, a, n_ctx)
        log_gamma = -jnp.exp(ret_decay[l].astype(F32))
        lg = jnp.repeat(log_gamma, RET_HEAD, axis=-1).reshape(2, 1, W)
        o_ret_dirs = _ret_call(rq, rk, P, (MAIN_COLS["ret"] + 2 * W) // W, lg, n_ctx)
        o_hg_dirs = _hgrn_call(hq, P, MAIN_COLS["hgrn_i"] // W, hk, hlf, n_ctx)

        merge_w = (row(rwkv_lnx_w[l]), row(rwkv_lnx_b[l]), row(hgrn_norm_w[l]), ones64, ones128,
                   w_branch[l].astype(BF16), w_out[l].astype(BF16))
        xm = _merge_call(xa, P, o_rw_dirs, g, bonus, o_ret_dirs, o_hg_dirs, merge_w, modsel, n_ctx, last)
        xa = _mlp_call(xm, modsel, norm2_w[l], mlp_w1[l].astype(BF16), mlp_w2[l].astype(BF16),
                       final_norm_w, n_ctx, (n_ctx // ROW_TILE) if last else 0, last)
        out = xa
    return out
```

```python
import functools

import jax
import jax.numpy as jnp
import numpy as np
from jax import lax
from jax.experimental import pallas as pl
from jax.experimental.pallas import tpu as pltpu

F32 = jnp.float32
BF16 = jnp.bfloat16

NORM_EPS = 1e-6
RWKV_LN_EPS = 64e-5
DECAY_SCALE = 0.6065306597126334
L2_EPS = 1e-12
ROPE_BASE = 10000.0
GRID_W = 64

BRANCH_W = 512
RWKV_HEAD = 64
RET_HEAD = 128
HGRN_HEAD = 128
DECAY_LORA = 64
AAA_LORA = 64
GATE_LORA = 128
LORA_W = 2 * DECAY_LORA + 2 * AAA_LORA + GATE_LORA
VDOWN_PAD = 128

RWKV_CHUNK = 64
RWKV_STEP = 128
RWKV_INV_BLOCK = 32
HGRN_CHUNK = 64
HGRN_SUB = 32
HGRN_STEP = 256
RET_CHUNK = 128
ROW_TILE = 256

VMEM_LIMIT = 56 * 1024 * 1024

D_MODEL = 2 * BRANCH_W
MAIN_COLS = dict(gate=0, rwkv=3 * D_MODEL, ret=3 * D_MODEL + 3 * BRANCH_W, hgrn_q=3 * D_MODEL + 7 * BRANCH_W,
                 hgrn_i=3 * D_MODEL + 8 * BRANCH_W, hgrn_g=3 * D_MODEL + 9 * BRANCH_W)
MAIN_WIDTH = 3 * D_MODEL + 10 * BRANCH_W
DECAY_COLS = dict(lora=0, v_down=LORA_W, hgrn_f=LORA_W + VDOWN_PAD)
DECAY_WIDTH = LORA_W + VDOWN_PAD + 2 * BRANCH_W


def _sigmoid(x):
    return 1.0 / (1.0 + jnp.exp(-x))


def _dot(a, b):
    return jnp.dot(a.astype(BF16), b.astype(BF16), preferred_element_type=F32)


def _dot_nt(a, b):
    return lax.dot_general(a.astype(BF16), b.astype(BF16), (((1,), (1,)), ((), ())),
                           preferred_element_type=F32)


def _dot_tn(a, b):
    return lax.dot_general(a.astype(BF16), b.astype(BF16), (((0,), (0,)), ((), ())),
                           preferred_element_type=F32)


def _cumsum_rows(mask_f32, x):
    m = mask_f32.astype(BF16)
    hi = x.astype(BF16)
    lo = (x - hi.astype(F32)).astype(BF16)
    return jnp.dot(m, hi, preferred_element_type=F32) + jnp.dot(m, lo, preferred_element_type=F32)


def _head_sum(x, ones_bd, split=True):
    hi = x.astype(BF16)
    out = jnp.dot(hi, ones_bd, preferred_element_type=F32)
    if split:
        lo = (x - hi.astype(F32)).astype(BF16)
        out = out + jnp.dot(lo, ones_bd, preferred_element_type=F32)
    return out


def _ada_kernel(c_ref, w_ref, b_ref, o_ref):
    cv = c_ref[...]
    s = cv * _sigmoid(cv)
    o_ref[0] = _dot(s, w_ref[0]) + b_ref[0]


def _ada_call(cvec, ada_w, ada_b):
    L, D, D6 = ada_w.shape
    tn = D6 // 4
    return pl.pallas_call(
        _ada_kernel,
        grid=(L, D6 // tn),
        in_specs=[pl.BlockSpec((8, D), lambda l, j: (0, 0)),
                  pl.BlockSpec((1, D, tn), lambda l, j: (l, 0, j)),
                  pl.BlockSpec((1, 1, tn), lambda l, j: (l, 0, j))],
        out_specs=pl.BlockSpec((1, 8, tn), lambda l, j: (l, 0, j)),
        out_shape=jax.ShapeDtypeStruct((L, 8, D6), F32),
        compiler_params=pltpu.CompilerParams(vmem_limit_bytes=VMEM_LIMIT),
        name="ada_mod",
    )(cvec, ada_w, ada_b.reshape(L, 1, D6))


def _modulated_norm(xf, nw, mod_ref, is_ctx, which):
    y = xf * lax.rsqrt(jnp.mean(xf * xf, axis=-1, keepdims=True) + NORM_EPS) * nw
    sh = jnp.where(is_ctx, mod_ref[0, which:which + 1, :], mod_ref[0, 6 + which:7 + which, :])
    sc = jnp.where(is_ctx, mod_ref[0, which + 1:which + 2, :], mod_ref[0, 7 + which:8 + which, :])
    return y * (1.0 + sc) + sh


def _is_ctx_rows(tile_idx, tm, n_ctx):
    row = tile_idx * tm + lax.broadcasted_iota(jnp.int32, (tm, 1), 0)
    return row < n_ctx


def _proj_kernel(x_ref, mod_ref, nw_ref, w_ref, o_ref, h_ref, *, tm, n_ctx):
    i = pl.program_id(1)

    @pl.when(pl.program_id(2) == 0)
    def _():
        h = _modulated_norm(x_ref[0], nw_ref[...], mod_ref, _is_ctx_rows(i, tm, n_ctx), 0)
        h_ref[...] = h.astype(BF16)

    o_ref[0] = jnp.dot(h_ref[...], w_ref[...], preferred_element_type=F32).astype(o_ref.dtype)


def _proj_call(xa, modsel, nw, w, n_ctx, out_dtype, n_tiles, tm=768):
    B, N, D = xa.shape
    NP = w.shape[1]
    tn = NP // n_tiles
    return pl.pallas_call(
        functools.partial(_proj_kernel, tm=tm, n_ctx=n_ctx),
        grid=(B, N // tm, n_tiles),
        in_specs=[pl.BlockSpec((1, tm, D), lambda b, i, j: (b, i, 0)),
                  pl.BlockSpec((1, 12, D), lambda b, i, j: (b, 0, 0)),
                  pl.BlockSpec((1, D), lambda b, i, j: (0, 0)),
                  pl.BlockSpec((D, tn), lambda b, i, j: (0, j))],
        out_specs=pl.BlockSpec((1, tm, tn), lambda b, i, j: (b, i, j)),
        out_shape=jax.ShapeDtypeStruct((B, N, NP), out_dtype),
        scratch_shapes=[pltpu.VMEM((tm, D), BF16)],
        compiler_params=pltpu.CompilerParams(
            dimension_semantics=("parallel", "parallel", "arbitrary"),
            vmem_limit_bytes=VMEM_LIMIT),
        name="in_proj",
    )(xa, modsel, nw.reshape(1, D), w)


def _shift_matrices(tm, seq):
    mats = np.zeros((5, tm, tm + 2 * GRID_W), np.float32)
    t = np.arange(tm)
    centre = t + GRID_W
    mats[0, t[1:], centre[1:] - 1] = 0.5
    mats[0, t[:-1], centre[:-1] + 1] = 0.5
    for kind, (first, last) in enumerate(((True, False), (False, False), (False, True), (True, True)), start=1):
        has_left = t % GRID_W != 0
        has_right = t % GRID_W != GRID_W - 1
        has_up = ~(first & (t < GRID_W))
        has_down = ~(last & (t >= tm - GRID_W))
        mats[kind, t[has_left], centre[has_left] - 1] = 0.25
        mats[kind, t[has_right], centre[has_right] + 1] = 0.25
        mats[kind, t[has_up], centre[has_up] - GRID_W] = 0.25
        mats[kind, t[has_down], centre[has_down] + GRID_W] = 0.25
    n_lat = seq // tm

    def kind_of_tile(i):
        latent = jnp.where(i == 1, 4 if n_lat == 1 else 1, jnp.where(i == n_lat, 3, 2))
        return jnp.where(i == 0, 0, latent)

    return jnp.asarray(mats, BF16), kind_of_tile


def _shift_mix(cur, prev, nxt, mu, s):
    ext = jnp.concatenate([prev, cur, nxt], axis=0)
    if ext.dtype == BF16:
        shifted = jnp.dot(s, ext, preferred_element_type=F32)
    else:
        hi = ext.astype(BF16)
        lo = (ext - hi.astype(F32)).astype(BF16)
        shifted = jnp.dot(s, hi, preferred_element_type=F32) + jnp.dot(s, lo, preferred_element_type=F32)
    cur = cur.astype(F32)
    return cur + mu * (shifted - cur)


def _prep_kernel(pc_ref, pp_ref, pn_ref, lc_ref, lp_ref, ln_ref, hv_ref, vf_ref, rq_ref, rk_ref, cs_ref, sn_ref,
                 hq_ref, hf0_ref, hf1_ref,
                 mum_ref, mul_ref, wup_ref, w0_ref, aup_ref, a0_ref, gup_ref, kk_w_ref, ka_ref, r_k_ref,
                 v0_ref, vup_ref, ones_ref, lb_ref,
                 shift_ref,
                 r_o, v_o, kk_o, lw_o, kd_o, a_o, g_o, bonus_o, rq_o, rk_o, hq_o, hk_o, hlf_o,
                 *, has_vres):
    W = BRANCH_W

    p = _shift_mix(pc_ref[0], pp_ref[0], pn_ref[0], mum_ref[...], shift_ref[0])
    lo = _shift_mix(lc_ref[0], lp_ref[0], ln_ref[0], mul_ref[...], shift_ref[0])
    r, k, v = p[:, :W], p[:, W:2 * W], p[:, 2 * W:]
    if has_vres:
        mix = _sigmoid(v0_ref[...] + _dot(hv_ref[0], vup_ref[...]))
        v = v + (vf_ref[0].astype(F32) - v) * mix
    r_o[0] = r.astype(r_o.dtype)
    v_o[0] = v.astype(v_o.dtype)
    gd = lo[:, 2 * DECAY_LORA + 2 * AAA_LORA:]
    g_o[0] = _dot(_sigmoid(gd), gup_ref[...]).astype(g_o.dtype)
    kx = k * kk_w_ref[...]
    kk_o[0] = (kx * lax.rsqrt(_head_sum(kx * kx, ones_ref[...], split=False) + L2_EPS)).astype(kk_o.dtype)
    k_sum = None
    for d in range(2):
        wd = lo[:, d * DECAY_LORA:(d + 1) * DECAY_LORA]
        ad = lo[:, 2 * DECAY_LORA + d * AAA_LORA:2 * DECAY_LORA + (d + 1) * AAA_LORA]
        z = -(w0_ref[d:d + 1, :] + _dot(jnp.tanh(wd), wup_ref[d]))
        lw_o[d, 0] = -DECAY_SCALE / (1.0 + jnp.exp(z))
        a = _sigmoid(a0_ref[d:d + 1, :] + _dot(ad, aup_ref[d]))
        a_o[d, 0] = a.astype(a_o.dtype)
        kd = k * (1.0 + (a - 1.0) * ka_ref[...])
        kd_o[d, 0] = kd.astype(kd_o.dtype)
        k_sum = kd if k_sum is None else k_sum + kd
    bonus_o[0] = (_head_sum(r * k_sum * r_k_ref[...], ones_ref[...], split=False) * v).astype(bonus_o.dtype)

    cosf = jnp.concatenate([cs_ref[...]] * (W // RET_HEAD), axis=1)
    sinf = jnp.concatenate([sn_ref[...]] * (W // RET_HEAD), axis=1)
    lane = lax.broadcasted_iota(jnp.int32, (1, W), 1)
    first_half = (lane & (RET_HEAD - 1)) < RET_HEAD // 2

    def rope(t):
        rot = jnp.where(first_half, pltpu.roll(t, W - RET_HEAD // 2, 1), pltpu.roll(t, RET_HEAD // 2, 1))
        return t * cosf + rot * sinf

    rq_o[0] = rope(rq_ref[0].astype(F32)).astype(rq_o.dtype)
    rk_o[0] = (rope(rk_ref[0].astype(F32)) * (RET_HEAD ** -0.5)).astype(rk_o.dtype)

    q = hq_ref[0].astype(F32)
    hq_o[0] = (q * _sigmoid(q)).astype(hq_o.dtype)
    for d, f_ref in enumerate((hf0_ref, hf1_ref)):
        f = f_ref[0]
        e = jnp.exp(-jnp.abs(f))
        inv = 1.0 / (1.0 + e)
        sig_pos = jnp.where(f >= 0, inv, e * inv)
        sig_neg = jnp.where(f >= 0, e * inv, inv)
        lb = lb_ref[d:d + 1, :]
        hlf_o[d, 0] = jnp.log(lb + (1.0 - lb) * sig_pos)
        hk_o[d, 0] = ((1.0 - lb) * sig_neg).astype(hk_o.dtype)


def _prep_call(P, Pd, v_first, tables, wts, n_ctx, has_vres):
    B, N, _ = P.shape
    W = BRANCH_W
    tm = ROW_TILE
    seq = N - n_ctx
    halo_per_tile = tm // GRID_W
    n_halo = N // GRID_W
    c_main = MAIN_COLS["rwkv"] // (3 * W)
    c_rq = MAIN_COLS["ret"] // W
    c_hq = MAIN_COLS["hgrn_q"] // W
    c_lora = DECAY_COLS["lora"] // LORA_W
    c_vd = DECAY_COLS["v_down"] // VDOWN_PAD
    c_hf = DECAY_COLS["hgrn_f"] // W

    def cur(width, col):
        return pl.BlockSpec((1, tm, width), lambda b, i: (b, i, col))

    def prev(width, col):
        return pl.BlockSpec((1, GRID_W, width), lambda b, i: (b, jnp.maximum(i * halo_per_tile - 1, 0), col))

    def nxt(width, col):
        return pl.BlockSpec((1, GRID_W, width),
                            lambda b, i: (b, jnp.minimum((i + 1) * halo_per_tile, n_halo - 1), col))

    def const(shape):
        return pl.BlockSpec(shape, lambda b, i: (0,) * len(shape))

    tok = pl.BlockSpec((1, tm, W), lambda b, i: (b, i, 0))
    tok2 = pl.BlockSpec((2, 1, tm, W), lambda b, i: (0, b, i, 0))
    cos, sin = tables
    in_specs = [cur(3 * W, c_main), prev(3 * W, c_main), nxt(3 * W, c_main),
                cur(LORA_W, c_lora), prev(LORA_W, c_lora), nxt(LORA_W, c_lora),
                cur(VDOWN_PAD, c_vd), tok, cur(W, c_rq), cur(W, c_rq + 1),
                pl.BlockSpec((tm, RET_HEAD), lambda b, i: (i, 0)),
                pl.BlockSpec((tm, RET_HEAD), lambda b, i: (i, 0)),
                cur(W, c_hq), cur(W, c_hf), cur(W, c_hf + 1)]
    in_specs += [const(w.shape) for w in wts]
    shift_mats, kind_of_tile = _shift_matrices(tm, seq)
    in_specs.append(pl.BlockSpec((1,) + shift_mats.shape[1:], lambda b, i: (kind_of_tile(i), 0, 0)))
    one = jax.ShapeDtypeStruct((B, N, W), BF16)
    two = jax.ShapeDtypeStruct((2, B, N, W), BF16)
    two_f32 = jax.ShapeDtypeStruct((2, B, N, W), F32)
    out_shape = [one, one, one, two_f32, two, two, one, one, one, one, one, two, two_f32]
    out_specs = [tok, tok, tok, tok2, tok2, tok2, tok, tok, tok, tok, tok, tok2, tok2]
    return pl.pallas_call(
        functools.partial(_prep_kernel, has_vres=has_vres),
        grid=(B, N // tm),
        in_specs=in_specs,
        out_specs=out_specs,
        out_shape=out_shape,
        compiler_params=pltpu.CompilerParams(
            dimension_semantics=("parallel", "parallel"), vmem_limit_bytes=VMEM_LIMIT),
        name="branch_prep",
    )(P, P, P, Pd, Pd, Pd, Pd, v_first, P, P, cos, sin, P, Pd, Pd, *wts, shift_mats)


def _merge_kernel(x_ref, g0_ref, g1_ref, g2_ref, orwf_ref, orwb_ref, g_ref, bonus_ref, oretf_ref, oretb_ref,
                  rg_ref, ohgf_ref, ohgb_ref, hg_ref,
                  lnw_ref, lnb_ref, hnw_ref, m64_ref, m128_ref, wb_ref, wo_ref, mod_ref, o_ref,
                  *, tm, n_ctx, tile_off):
    i = pl.program_id(1) + tile_off

    o_rw = orwf_ref[0].astype(F32) + orwb_ref[0].astype(F32)
    o_ret = oretf_ref[0].astype(F32) + oretb_ref[0].astype(F32)
    o_hg = ohgf_ref[0].astype(F32) + ohgb_ref[0].astype(F32)
    sum_rw = _head_sum(o_rw, m64_ref[...])
    ms_ret = _head_sum(o_ret * o_ret, m128_ref[...], split=False) * (1.0 / RET_HEAD)
    ms_hg = _head_sum(o_hg * o_hg, m128_ref[...], split=False) * (1.0 / HGRN_HEAD)
    cen = o_rw - sum_rw * (1.0 / RWKV_HEAD)
    var = _head_sum(cen * cen, m64_ref[...], split=False) * (1.0 / RWKV_HEAD)
    rg = rg_ref[0].astype(F32)
    y_ret = o_ret * lax.rsqrt(ms_ret + NORM_EPS) * (rg * _sigmoid(rg))
    hg = hg_ref[0].astype(F32)
    y_hg = o_hg * lax.rsqrt(ms_hg + NORM_EPS) * hnw_ref[...] * (hg * _sigmoid(hg))
    y_rw = ((cen * lax.rsqrt(var + RWKV_LN_EPS) * lnw_ref[...] + lnb_ref[...] + bonus_ref[0].astype(F32))
            * g_ref[0].astype(F32))

    merged = None
    for b, y, gate_ref in ((1, y_ret, g1_ref), (2, y_hg, g2_ref), (0, y_rw, g0_ref)):
        z = _sigmoid(gate_ref[0].astype(F32)) * _dot(y, wb_ref[b])
        merged = z if merged is None else merged + z
    out = _dot(merged, wo_ref[...])
    g1 = jnp.where(_is_ctx_rows(i, tm, n_ctx), mod_ref[0, 2:3, :], mod_ref[0, 8:9, :])
    o_ref[0] = x_ref[0] + g1 * out


def _merge_call(xa, P, o_rw, g, bonus, o_ret, o_hg, wts, modsel, n_ctx, latent_only):
    B, N, D = xa.shape
    W = BRANCH_W
    tm = ROW_TILE
    off = n_ctx // tm if latent_only else 0
    n_rows = N - off * tm
    c_gate = MAIN_COLS["gate"] // D
    c_rg = (MAIN_COLS["ret"] + 3 * W) // W
    c_hg = MAIN_COLS["hgrn_g"] // W

    def col(width, c):
        return pl.BlockSpec((1, tm, width), lambda b, i: (b, i + off, c))

    def const(shape):
        return pl.BlockSpec(shape, lambda b, i: (0,) * len(shape))

    tok = col(W, 0)
    return pl.pallas_call(
        functools.partial(_merge_kernel, tm=tm, n_ctx=n_ctx, tile_off=off),
        grid=(B, n_rows // tm),
        in_specs=[col(D, 0), col(D, c_gate), col(D, c_gate + 1), col(D, c_gate + 2),
                  tok, tok, tok, tok, tok, tok, col(W, c_rg), tok, tok, col(W, c_hg)]
                 + [const(w.shape) for w in wts]
                 + [pl.BlockSpec((1, 12, D), lambda b, i: (b, 0, 0))],
        out_specs=pl.BlockSpec((1, tm, D), lambda b, i: (b, i, 0)),
        out_shape=jax.ShapeDtypeStruct((B, n_rows, D), F32),
        compiler_params=pltpu.CompilerParams(
            dimension_semantics=("parallel", "parallel"), vmem_limit_bytes=VMEM_LIMIT),
        name="merge_out",
    )(xa, P, P, P, o_rw[0], o_rw[1], g, bonus, o_ret[0], o_ret[1], P, o_hg[0], o_hg[1], P, *wts, modsel)


def _mlp_kernel(x_ref, mod_ref, nw_ref, w1_ref, w2_ref, fw_ref, o_ref, *, tm, n_ctx, tile_off, final):
    i = pl.program_id(1) + tile_off
    is_ctx = _is_ctx_rows(i, tm, n_ctx)
    xf = x_ref[0]
    h = _modulated_norm(xf, nw_ref[...], mod_ref, is_ctx, 3)
    a = jnp.maximum(_dot(h, w1_ref[...]), 0.0)
    out = _dot(a * a, w2_ref[...])
    g2 = jnp.where(is_ctx, mod_ref[0, 5:6, :], mod_ref[0, 11:12, :])
    xn = xf + g2 * out
    if final:
        xn = xn * lax.rsqrt(jnp.mean(xn * xn, axis=-1, keepdims=True) + NORM_EPS) * fw_ref[...]
    o_ref[0] = xn


def _mlp_call(xa, modsel, nw, w1, w2, fw, n_ctx, tile_off, final):
    B, n_rows, D = xa.shape
    DF = w1.shape[1]
    tm = ROW_TILE
    return pl.pallas_call(
        functools.partial(_mlp_kernel, tm=tm, n_ctx=n_ctx, tile_off=tile_off, final=final),
        grid=(B, n_rows // tm),
        in_specs=[pl.BlockSpec((1, tm, D), lambda b, i: (b, i, 0)),
                  pl.BlockSpec((1, 12, D), lambda b, i: (b, 0, 0)),
                  pl.BlockSpec((1, D), lambda b, i: (0, 0)),
                  pl.BlockSpec((D, DF), lambda b, i: (0, 0)),
                  pl.BlockSpec((DF, D), lambda b, i: (0, 0)),
                  pl.BlockSpec((1, D), lambda b, i: (0, 0))],
        out_specs=pl.BlockSpec((1, tm, D), lambda b, i: (b, i, 0)),
        out_shape=jax.ShapeDtypeStruct((B, n_rows, D), F32),
        compiler_params=pltpu.CompilerParams(
            dimension_semantics=("parallel", "parallel"), vmem_limit_bytes=VMEM_LIMIT),
        name="mlp",
    )(xa, modsel, nw.reshape(1, D), w1, w2, fw.reshape(1, D))


def _order_masks(d, n):
    t = lax.broadcasted_iota(jnp.int32, (n, n), 0)
    s = lax.broadcasted_iota(jnp.int32, (n, n), 1)
    lead = (t - s) * (1 - 2 * d)
    return lead > 0, lead >= 0


def _scan_specs(T, W, n_ctx, N, B):
    n_ctx_blocks, n_blocks = n_ctx // T, N // T

    def block(d, j):
        if d == 0:
            return j
        return jnp.where(j < n_ctx_blocks, n_ctx_blocks - 1 - j, n_blocks - 1 + n_ctx_blocks - j)

    def shared(d, col=0):
        return pl.BlockSpec((B, T, W), lambda j: (0, block(d, j), col))

    def per_dir(d):
        return pl.BlockSpec((1, B, T, W), lambda j: (d, 0, block(d, j), 0))

    return shared, per_dir


def _scan_call(kernel_fn, name, T, specs, args, state_shape, n_ctx):
    B, N, W = args[0].shape[0], args[0].shape[1], BRANCH_W
    shared, _ = _scan_specs(T, W, n_ctx, N, B)
    out = jax.ShapeDtypeStruct((B, N, W), BF16)
    return pl.pallas_call(
        kernel_fn,
        grid=(N // T,),
        in_specs=specs,
        out_specs=[shared(0), shared(1)],
        out_shape=[out, out],
        scratch_shapes=[pltpu.VMEM((B,) + state_shape, F32)],
        compiler_params=pltpu.CompilerParams(
            dimension_semantics=("arbitrary",), vmem_limit_bytes=VMEM_LIMIT),
        name=name,
    )(*args)


def _load(ref, b, rows=slice(None)):
    x = ref[b, rows, :] if len(ref.shape) == 3 else ref[0, b, rows, :]
    return x.astype(F32)


def _rwkv_kernel(*refs):
    sides, (of_ref, ob_ref, s_ref) = (refs[0:6], refs[6:12]), refs[12:]
    n_batch, _, n_groups = s_ref.shape[:3]

    @pl.when(pl.program_id(0) == 0)
    def _():
        s_ref[...] = jnp.zeros_like(s_ref)

    C = RWKV_CHUNK
    n_sub = RWKV_STEP // C
    def rows(d, i):
        c = i if d == 0 else n_sub - 1 - i
        return slice(c * C, (c + 1) * C)

    seqs = [(b, d, i) for b in range(n_batch) for d in range(2) for i in range(n_sub)]
    ins = {(b, d, i): tuple(_load(ref, b, rows(d, i)) for ref in sides[d]) for b, d, i in seqs}
    pre = _rwkv_pre(seqs, ins, n_groups)
    S = {(b, d, g): s_ref[b, d, g] for b in range(n_batch) for d in range(2) for g in range(n_groups)}
    for i in range(n_sub):
        now = [sq for sq in seqs if sq[2] == i]
        outs, S = _rwkv_apply(now, pre, S, n_groups)
        for b, d, _ in now:
            (of_ref, ob_ref)[d][b, rows(d, i), :] = outs[b, d, i].astype(BF16)
    for key, s_new in S.items():
        s_ref[key] = s_new


def _rwkv_pre(seqs, ins, n_groups):
    C = RWKV_CHUNK
    HG = 4 * RWKV_HEAD
    chains = [sq + (g,) for sq in seqs for g in range(n_groups)]

    rb = lax.broadcasted_iota(jnp.int32, (4 * C, HG), 0) // C
    lb = lax.broadcasted_iota(jnp.int32, (4 * C, HG), 1) // RWKV_HEAD
    same_head = rb == lb
    head_mask = same_head.astype(F32).astype(BF16)
    t_idx = lax.broadcasted_iota(jnp.int32, (C, HG), 0)
    s_idx = lax.broadcasted_iota(jnp.int32, (C, HG), 1) % C

    def bd(x):
        xb = x.astype(BF16)
        return jnp.concatenate([xb, xb, xb, xb], axis=0) * head_mask

    incl_c = [_order_masks(d, C)[1].astype(F32) for d in range(2)]
    G = {sq: _cumsum_rows(incl_c[sq[1]], ins[sq][3]) for sq in seqs}
    strict, incl, eye = {}, {}, {}
    for d in range(2):
        lead = (t_idx - s_idx) * (1 - 2 * d)
        strict[d], incl[d], eye[d] = lead > 0, lead >= 0, (lead == 0).astype(F32)
    q_t, r_t, p_t, k_t, p_hat, k_hat, v_g, e_end = {}, {}, {}, {}, {}, {}, {}, {}
    for sq in seqs:
        r, v, kk, lw, k, a = ins[sq]
        g_end = jnp.sum(lw, axis=0, keepdims=True)
        e_neg = jnp.exp(-G[sq])
        end = jnp.exp(g_end)
        full = dict(q=kk * jnp.exp(G[sq] - lw), r=r * jnp.exp(G[sq]), p=-(a * kk) * e_neg, k=k * e_neg, v=v)
        for g in range(n_groups):
            sl = slice(g * HG, (g + 1) * HG)
            c = sq + (g,)
            q_t[c], r_t[c], p_t[c], k_t[c], v_g[c] = (full[n][:, sl] for n in "qrpkv")
            e_end[c] = end[:, sl]
            p_hat[c], k_hat[c] = p_t[c] * e_end[c], k_t[c] * e_end[c]

    bv = {c: bd(v_g[c]) for c in chains}
    qr = {c: jnp.concatenate([q_t[c], r_t[c]], axis=0).astype(BF16) for c in chains}
    a_p = {c: _dot_nt(qr[c], bd(p_t[c])) for c in chains}
    a_k = {c: _dot_nt(qr[c], bd(k_t[c])) for c in chains}
    n_qp = {c: jnp.where(strict[c[1]], a_p[c][:C], 0.0) for c in chains}
    a_qk = {c: jnp.where(strict[c[1]], a_k[c][:C], 0.0) for c in chains}
    a_rp = {c: jnp.where(incl[c[1]], a_p[c][C:], 0.0) for c in chains}
    a_rk = {c: jnp.where(incl[c[1]], a_k[c][C:], 0.0) for c in chains}

    on_diag = (t_idx // RWKV_INV_BLOCK) == (s_idx // RWKV_INV_BLOCK)
    n_d = {c: jnp.where(on_diag, n_qp[c], 0.0) for c in chains}
    n_off = {c: n_qp[c] - n_d[c] for c in chains}
    R = {c: _dot(n_d[c], bd(n_d[c])) for c in chains}
    T = {c: eye[c[1]] + n_d[c] for c in chains}
    avk = {c: _dot(jnp.concatenate([a_qk[c], a_rk[c]], axis=0), bv[c]) for c in chains}
    span = 4
    while span < RWKV_INV_BLOCK:
        RT = {c: _dot(jnp.concatenate([R[c], T[c]], axis=0), bd(R[c])) for c in chains}
        R = {c: RT[c][:C] for c in chains}
        T = {c: T[c] + RT[c][C:] for c in chains}
        span *= 2
    Td = {c: T[c] + _dot(T[c], bd(R[c])) for c in chains}
    tn = {c: _dot(Td[c], bd(n_off[c])) for c in chains}
    Tb = {c: (Td[c] + _dot(tn[c], bd(Td[c]))).astype(BF16) for c in chains}

    w = {c: _dot(Tb[c], bd(q_t[c])) for c in chains}
    u0 = {c: _dot(Tb[c], bd(avk[c][:C])) for c in chains}
    wr = {c: jnp.concatenate([w[c], r_t[c]], axis=0).astype(BF16) for c in chains}
    pk_hat = {c: jnp.concatenate([p_hat[c], k_hat[c]], axis=0).astype(BF16) for c in chains}
    return dict(bd=bd, same_head=same_head, wr=wr, u0=u0, a_rp=a_rp, o_v={c: avk[c][C:] for c in chains},
                pk_hat=pk_hat, v=v_g, e_end=e_end)


def _rwkv_apply(seqs, pre, S, n_groups):
    C = RWKV_CHUNK
    bd, same_head = pre["bd"], pre["same_head"]
    chains = [sq + (g,) for sq in seqs for g in range(n_groups)]
    state_of = {c: (c[0], c[1], c[3]) for c in chains}
    Sb = {c: S[state_of[c]].astype(BF16) for c in chains}
    ws = {c: _dot_nt(pre["wr"][c], Sb[c]) for c in chains}
    u = {c: ws[c][:C] + pre["u0"][c] for c in chains}
    o = {c: _dot(pre["a_rp"][c], bd(u[c])) + pre["o_v"][c] + ws[c][C:] for c in chains}
    upd = {c: _dot_tn(jnp.concatenate([u[c], pre["v"][c]], axis=0), pre["pk_hat"][c]) for c in chains}
    S_new = dict(S)
    for c in chains:
        S_new[state_of[c]] = S[state_of[c]] * pre["e_end"][c] + jnp.where(same_head, upd[c], 0.0)
    outs = {sq: jnp.concatenate([o[sq + (g,)] for g in range(n_groups)], axis=1) for sq in seqs}
    return outs, S_new


def _rwkv_call(r, v, kk, lw, k, a, n_ctx):
    B, N, W = r.shape
    T = RWKV_STEP
    assert RWKV_CHUNK == RWKV_HEAD and W % (4 * RWKV_HEAD) == 0
    shared, per_dir = _scan_specs(T, W, n_ctx, N, B)
    specs = [s for d in range(2) for s in (shared(d), shared(d), shared(d), per_dir(d), per_dir(d), per_dir(d))]
    args = (r, v, kk, lw, k, a) * 2
    n_groups = W // (4 * RWKV_HEAD)
    return _scan_call(_rwkv_kernel, "rwkv7_scan", T, specs, args,
                      (2, n_groups, 4 * RWKV_HEAD, 4 * RWKV_HEAD), n_ctx)


def _ret_kernel(qf_ref, kf_ref, vf_ref, qb_ref, kb_ref, vb_ref, lg_ref, of_ref, ob_ref, s_ref):
    C = RET_CHUNK
    Dh = RET_HEAD

    @pl.when(pl.program_id(0) == 0)
    def _():
        s_ref[...] = jnp.zeros_like(s_ref)

    t = lax.broadcasted_iota(jnp.int32, (C, C), 0)
    s = lax.broadcasted_iota(jnp.int32, (C, C), 1)
    dist = jnp.abs(t - s).astype(F32)
    tcol = lax.broadcasted_iota(jnp.int32, (C, 1), 0)
    io = ((qf_ref, kf_ref, vf_ref, of_ref), (qb_ref, kb_ref, vb_ref, ob_ref))
    n_batch = s_ref.shape[0]
    n_heads = BRANCH_W // Dh
    chains = [(b, d, h) for b in range(n_batch) for d in range(2) for h in range(n_heads)]
    incl = [_order_masks(d, C)[1] for d in range(2)]
    n_t = [(tcol + 1 if d == 0 else C - tcol).astype(F32) for d in range(2)]

    def head(ref, b, h):
        return ref[b, :, h * Dh:(h + 1) * Dh].astype(F32)

    scores = {(b, d, h): _dot_nt(head(io[d][0], b, h), head(io[d][1], b, h)) for b, d, h in chains}
    outs, H = {}, {}
    for b, d, h in chains:
        lg = lg_ref[d, :, h * Dh:h * Dh + C]
        lgd = lg_ref[d, :, h * Dh:(h + 1) * Dh]
        A = scores[b, d, h] * jnp.where(incl[d], jnp.exp(lg * dist), 0.0)
        H[b, d, h] = s_ref[b, d, h]
        outs[b, d, h] = (_dot(A, head(io[d][2], b, h))
                         + _dot(head(io[d][0], b, h) * jnp.exp(lgd * n_t[d]), H[b, d, h]))
    for b, d, h in chains:
        lgd = lg_ref[d, :, h * Dh:(h + 1) * Dh]
        k_hat = head(io[d][1], b, h) * jnp.exp(lgd * (C - n_t[d]))
        s_ref[b, d, h] = H[b, d, h] * jnp.exp(lgd * C) + _dot_tn(k_hat, head(io[d][2], b, h))
    for b in range(n_batch):
        for d in range(2):
            io[d][3][b] = jnp.concatenate([outs[b, d, h] for h in range(n_heads)], axis=1).astype(BF16)


def _ret_call(q, k, v, v_col, lg, n_ctx):
    B, N, W = q.shape
    C = RET_CHUNK
    assert C <= RET_HEAD
    shared, _ = _scan_specs(C, W, n_ctx, N, B)
    specs = [s for d in range(2) for s in (shared(d), shared(d), shared(d, v_col))]
    specs.append(pl.BlockSpec((2, 1, W), lambda j: (0, 0, 0)))
    return _scan_call(_ret_kernel, "retention_scan", C, specs, (q, k, v, q, k, v, lg),
                      (2, W // RET_HEAD, RET_HEAD, RET_HEAD), n_ctx)


def _hgrn_kernel(qf_ref, vf_ref, kf_ref, lff_ref, qb_ref, vb_ref, kb_ref, lfb_ref, of_ref, ob_ref, s_ref):
    C = HGRN_CHUNK
    n_batch, _, n_heads = s_ref.shape[:3]

    @pl.when(pl.program_id(0) == 0)
    def _():
        s_ref[...] = jnp.zeros_like(s_ref)

    io = ((qf_ref, vf_ref, kf_ref, lff_ref, of_ref), (qb_ref, vb_ref, kb_ref, lfb_ref, ob_ref))
    n_chunks = HGRN_STEP // C
    order = [list(range(n_chunks)), list(reversed(range(n_chunks)))]
    seqs = [(b, d) for b in range(n_batch) for d in range(2)]
    blocks = [sq + (c,) for sq in seqs for c in range(n_chunks)]
    incl_c = [_order_masks(d, C)[1].astype(F32) for d in range(2)]

    def rows(ref, b, c):
        x = ref[b, c * C:(c + 1) * C, :] if len(ref.shape) == 3 else ref[0, b, c * C:(c + 1) * C, :]
        return x.astype(F32)

    G = {(b, d, c): _cumsum_rows(incl_c[d], rows(io[d][3], b, c)) for b, d, c in blocks}
    pre = {}
    for b, d, c in blocks:
        q_ref, v_ref, k_ref, lf_ref, _ = io[d]
        pre[b, d, c] = _hgrn_intra(d, rows(q_ref, b, c), rows(k_ref, b, c), rows(v_ref, b, c),
                                   rows(lf_ref, b, c), G[b, d, c])
    heads = range(n_heads)
    intra = {blk + (h,): _dot(pre[blk]["A"][h], pre[blk]["v"][h]) for blk in blocks for h in heads}

    S = {sq + (h,): s_ref[sq + (h,)] for sq in seqs for h in heads}
    for i in range(n_chunks):
        inter, S_next = {}, {}
        for b, d in seqs:
            blk = pre[b, d, order[d][i]]
            for h in heads:
                inter[b, d, h] = _dot_nt(blk["q_full"][h], S[b, d, h])
                S_next[b, d, h] = S[b, d, h] * blk["e_end"][h] + _dot_tn(blk["v"][h], blk["k_hat"][h])
        for b, d in seqs:
            c = order[d][i]
            io[d][4][b, c * C:(c + 1) * C, :] = jnp.concatenate(
                [intra[b, d, c, h] + inter[b, d, h] for h in heads], axis=1).astype(BF16)
        S = S_next
    for key, s_new in S.items():
        s_ref[key] = s_new


def _hgrn_intra(d, q, k, v, lf, G):
    C, SUB = HGRN_CHUNK, HGRN_SUB
    Dh = HGRN_HEAD
    n_sub = C // SUB
    _, incl = _order_masks(d, C)
    g_end = jnp.sum(lf, axis=0, keepdims=True)
    W = q.shape[-1]
    anchors = [G[I * SUB + SUB // 2:I * SUB + SUB // 2 + 1, :] for I in range(n_sub)]
    g_anchor = jnp.concatenate([jnp.broadcast_to(a, (SUB, W)) for a in anchors], axis=0)
    q_a = q * jnp.exp(G - g_anchor)
    q_full = q * jnp.exp(G)
    k_hat = k * jnp.exp(g_end - G)
    sub_of_row = lax.broadcasted_iota(jnp.int32, (C, 1), 0) // SUB
    k_anch = []
    for I in range(n_sub):
        visible = (I - sub_of_row) * (1 - 2 * d) >= 0
        k_anch.append(k * jnp.exp(jnp.where(visible, anchors[I] - G, 0.0)))
    e_end = jnp.exp(g_end)
    out = dict(A=[], v=[], q_full=[], k_hat=[], e_end=[])
    for h in range(W // Dh):
        sl = slice(h * Dh, (h + 1) * Dh)
        A = jnp.concatenate([_dot_nt(q_a[I * SUB:(I + 1) * SUB, sl], k_anch[I][:, sl]) for I in range(n_sub)],
                            axis=0)
        out["A"].append(jnp.where(incl, A, 0.0))
        out["v"].append(v[:, sl])
        out["q_full"].append(q_full[:, sl])
        out["k_hat"].append(k_hat[:, sl])
        out["e_end"].append(e_end[:, sl])
    return out


def _hgrn_call(q, v, v_col, k, lf, n_ctx):
    B, N, W = q.shape
    T = HGRN_STEP
    shared, per_dir = _scan_specs(T, W, n_ctx, N, B)
    specs = [s for d in range(2) for s in (shared(d), shared(d, v_col), per_dir(d), per_dir(d))]
    return _scan_call(_hgrn_kernel, "hgrn2_scan", T, specs, (q, v, k, lf) * 2,
                      (2, W // HGRN_HEAD, HGRN_HEAD, HGRN_HEAD), n_ctx)


def _rope_tables(n, head):
    half = head // 2
    inv_freq = ROPE_BASE ** (-jnp.arange(half, dtype=F32) / half)
    ang = jnp.arange(n).astype(F32)[:, None] * inv_freq[None, :]
    cos, sin = jnp.cos(ang), jnp.sin(ang)
    return jnp.concatenate([cos, cos], axis=1), jnp.concatenate([-sin, sin], axis=1)


def _block_diag_ones(width, head):
    idx = jnp.arange(width) // head
    return (idx[:, None] == idx[None, :]).astype(BF16)


def _permute_w_in(w_in_l, v_down_l):
    D = w_in_l.shape[0]
    W = BRANCH_W
    n_rw, n_ret, n_hg = 3 * W + LORA_W, 4 * W, 5 * W
    rw = w_in_l[:, :n_rw]
    ret = w_in_l[:, n_rw:n_rw + n_ret]
    hg = w_in_l[:, n_rw + n_ret:n_rw + n_ret + n_hg]
    gate = w_in_l[:, n_rw + n_ret + n_hg:]
    extra = jnp.zeros((D, VDOWN_PAD), F32)
    if v_down_l is not None:
        extra = extra.at[:, :v_down_l.shape[1]].set(v_down_l)
    main = jnp.concatenate([gate, rw[:, :3 * W], ret, hg[:, :W], hg[:, 3 * W:]], axis=1)
    decay = jnp.concatenate([rw[:, 3 * W:], extra, hg[:, W:3 * W]], axis=1)
    assert main.shape[1] == MAIN_WIDTH and decay.shape[1] == DECAY_WIDTH
    return main.astype(BF16), decay.astype(BF16)


def kernel(x, c, ctx, c_ctx, ada_w, ada_b, norm1_w, norm2_w, w_in, rwkv_mu, rwkv_w0, rwkv_w_up, rwkv_a0, rwkv_a_up, rwkv_g_up, rwkv_k_k, rwkv_k_a, rwkv_r_k, rwkv_lnx_w, rwkv_lnx_b, rwkv_v0, rwkv_v_down, rwkv_v_up, ret_decay, hgrn_lb, hgrn_norm_w, w_branch, w_out, mlp_w1, mlp_w2, final_norm_w):
    B, SEQ, D = x.shape
    n_ctx = ctx.shape[1]
    N = n_ctx + SEQ
    L = ada_w.shape[0]
    W = BRANCH_W
    assert n_ctx == ROW_TILE and SEQ % ROW_TILE == 0 and D == D_MODEL
    assert MAIN_COLS["rwkv"] % (3 * W) == 0 and MAIN_COLS["ret"] % W == 0

    xa = jnp.concatenate([ctx, x], axis=1)
    cvec = jnp.concatenate([c, c_ctx[None], jnp.zeros((8 - B - 1, D), F32)], axis=0)
    mod = _ada_call(cvec, ada_w, ada_b)
    sm = jax.nn.softmax(hgrn_lb.astype(F32), axis=0)
    hgrn_lower = jnp.cumsum(sm, axis=0) - sm[0:1]
    tables = _rope_tables(N, RET_HEAD)
    ones64 = _block_diag_ones(W, RWKV_HEAD)
    ones128 = _block_diag_ones(W, RET_HEAD)
    row = lambda t: t.reshape(1, -1)
    v_first = None
    out = None

    for l in range(L):
        last = l == L - 1
        mod_c = jnp.broadcast_to(mod[l, B].reshape(1, 6, D), (B, 6, D))
        modsel = jnp.concatenate([mod_c, mod[l, :B].reshape(B, 6, D)], axis=1)
        w_main, w_decay = _permute_w_in(w_in[l], None if l == 0 else rwkv_v_down[l - 1])
        P = _proj_call(xa, modsel, norm1_w[l], w_main, n_ctx, BF16, 4)
        Pd = _proj_call(xa, modsel, norm1_w[l], w_decay, n_ctx, F32, 1)

        has_vres = l > 0
        if has_vres:
            v0 = row(rwkv_v0[l - 1])
            v_up = jnp.zeros((VDOWN_PAD, W), F32).at[:rwkv_v_up.shape[1]].set(rwkv_v_up[l - 1]).astype(BF16)
        else:
            v0 = jnp.zeros((1, W), F32)
            v_up = jnp.zeros((VDOWN_PAD, W), BF16)
        prep_w = (row(rwkv_mu[l, :3 * W]), row(rwkv_mu[l, 3 * W:]),
                  rwkv_w_up[l].astype(BF16), rwkv_w0[l], rwkv_a_up[l].astype(BF16), rwkv_a0[l],
                  rwkv_g_up[l].astype(BF16), row(rwkv_k_k[l]), row(rwkv_k_a[l]), row(rwkv_r_k[l]),
                  v0, v_up, ones64, hgrn_lower[l])
        (r, v, kk, lw, kd, a, g, bonus, rq, rk, hq, hk, hlf) = _prep_call(
            P, Pd, P if v_first is None else v_first, tables, prep_w, n_ctx, has_vres)
        if v_first is None:
            v_first = v

        o_rw_dirs = _rwkv_call(r, v, kk, lw, kd, a, n_ctx)
        log_gamma = -jnp.exp(ret_decay[l].astype(F32))
        lg = jnp.repeat(log_gamma, RET_HEAD, axis=-1).reshape(2, 1, W)
        o_ret_dirs = _ret_call(rq, rk, P, (MAIN_COLS["ret"] + 2 * W) // W, lg, n_ctx)
        o_hg_dirs = _hgrn_call(hq, P, MAIN_COLS["hgrn_i"] // W, hk, hlf, n_ctx)

        merge_w = (row(rwkv_lnx_w[l]), row(rwkv_lnx_b[l]), row(hgrn_norm_w[l]), ones64, ones128,
                   w_branch[l].astype(BF16), w_out[l].astype(BF16))
        xm = _merge_call(xa, P, o_rw_dirs, g, bonus, o_ret_dirs, o_hg_dirs, merge_w, modsel, n_ctx, last)
        xa = _mlp_call(xm, modsel, norm2_w[l], mlp_w1[l].astype(BF16), mlp_w2[l].astype(BF16),
                       final_norm_w, n_ctx, (n_ctx // ROW_TILE) if last else 0, last)
        out = xa
    return out
```
